```python
import jax
import jax.numpy as jnp
from jax import lax
import numpy as np

D_MODEL = 2048
BATCH = 2
SEQ = 4096
DEPTH = 4
DEC_BATCH = 128
DEC_SEQ = 8
PAST_LEN = 8192
PAGE_SIZE = 128

N_MIXERS = 4
ROPE_THETA = 10000.0
NORM_EPS = 1e-6
BLOCK = 128
D_FF = ((8 * D_MODEL + 3 * 256 - 1) // (3 * 256)) * 256

A_HEADS = 16
A_HEAD_DIM = 128
A_KV_HEADS = 4
A_IDX_HEADS = 16
A_IDX_DIM = 128
A_TOPK = 256
A_IDX_SCALE = (A_IDX_HEADS * A_IDX_DIM) ** -0.5
A_IN = A_HEADS * A_HEAD_DIM + 2 * A_KV_HEADS * A_HEAD_DIM + A_IDX_HEADS * A_IDX_DIM + A_IDX_DIM + A_IDX_HEADS

B_HEADS = 32
B_HEAD_DIM = 64
B_KV_HEADS = 4
B_WINDOW = 128
B_IN = (B_HEADS + 2 * B_KV_HEADS) * B_HEAD_DIM

C_GROUPS = ((128, 1), (512, 4), (2048, 16))
C_HEADS = 8
C_KV_HEADS = 2
C_HEAD_DIM = 128
C_GROUP_IN = (C_HEADS + 2 * C_KV_HEADS) * C_HEAD_DIM
C_IN = len(C_GROUPS) * C_GROUP_IN

D_HEADS = 16
D_NOPE = 128
D_ROPE = 64
D_V = 128
D_Q_LORA = 512
D_KV_LORA = 512
D_IN = D_Q_LORA + D_KV_LORA + D_ROPE
MLA_SCALE = (D_NOPE + D_ROPE) ** -0.5

kernel_name = 'hybrid_dsa_swa_dilated_mla_step'


def split_last(x, sizes):
    out, start = [], 0
    for s in sizes:
        out.append(x[..., start:start + s])
        start += s
    return out


def rms_norm(x, g):
    xf = x.astype(jnp.float32)
    y = xf * lax.rsqrt(jnp.mean(xf * xf, axis=-1, keepdims=True) + NORM_EPS)
    return (y * g.astype(jnp.float32)).astype(x.dtype)


def rope(x, pos):
    dh = x.shape[-1]
    inv = jnp.power(ROPE_THETA, -jnp.arange(0, dh, 2, dtype=jnp.float32) / dh)
    ang = pos.astype(jnp.float32)[:, None] * inv[None, :]
    cos = jnp.cos(ang)[:, None, :]
    sin = jnp.sin(ang)[:, None, :]
    xf = x.astype(jnp.float32)
    x1, x2 = xf[..., : dh // 2], xf[..., dh // 2:]
    return jnp.concatenate([x1 * cos - x2 * sin, x2 * cos + x1 * sin], axis=-1).astype(x.dtype)


def masked_softmax(s, mask, sink=None):
    s = jnp.where(mask, s, -jnp.inf)
    m = jnp.max(s, axis=-1, keepdims=True)
    if sink is not None:
        m = jnp.maximum(m, sink)
    m = jnp.where(jnp.isfinite(m), m, 0.0)
    e = jnp.exp(s - m)
    den = jnp.sum(e, axis=-1, keepdims=True)
    if sink is not None:
        den = den + jnp.exp(sink - m)
    return e / den, (m + jnp.log(den))[..., 0]


def swiglu(x, w_in, w_out):
    g, u = jnp.split(x @ w_in, 2, axis=-1)
    return (jax.nn.silu(g) * u) @ w_out


def group_q(q, kv_heads):
    n, t, h, d = q.shape
    return q.reshape(n, t, kv_heads, h // kv_heads, d)


def to_query_blocks(a):
    n, s = a.shape[:2]
    return jnp.moveaxis(a.reshape(n, s // BLOCK, BLOCK, *a.shape[2:]), 1, 0)


def from_query_blocks(o):
    nb, n, blk = o.shape[:3]
    return jnp.moveaxis(o, 0, 1).reshape(n, nb * blk, *o.shape[3:])


def banded_window_attention(q, k, v, window, sink=None):
    n, L, kvh, g, dh = q.shape
    nb = -(-L // BLOCK)
    pad = nb * BLOCK - L

    def padl(a):
        return jnp.pad(a, [(0, 0), (0, pad)] + [(0, 0)] * (a.ndim - 2))

    qb = padl(q).reshape(n, nb, BLOCK, kvh, g, dh)
    kb = padl(k).reshape(n, nb, BLOCK, kvh, dh)
    vb = padl(v).reshape(n, nb, BLOCK, kvh, dh)

    def with_prev(a):
        prev = jnp.concatenate([jnp.zeros_like(a[:, :1]), a[:, :-1]], axis=1)
        return jnp.concatenate([prev, a], axis=2)

    kw, vw = with_prev(kb), with_prev(vb)
    s = jnp.einsum('ncqkgd,ncskd->nckgqs', qb, kw).astype(jnp.float32) * dh ** -0.5
    qi = jnp.arange(BLOCK)[:, None]
    sj = jnp.arange(2 * BLOCK)[None, :]
    dist = qi + BLOCK - sj
    kpos = jnp.arange(nb)[:, None, None] * BLOCK - BLOCK + sj
    mask = ((dist >= 0) & (dist <= window))[None] & (kpos >= 0)
    sk = None if sink is None else sink.astype(jnp.float32)[None, None, :, :, None, None]
    p, lse = masked_softmax(s, mask[None, :, None, None], sk)
    o = jnp.einsum('nckgqs,ncskd->ncqkgd', p.astype(v.dtype), vw)
    o = o.reshape(n, nb * BLOCK, kvh, g, dh)[:, :L]
    lse = jnp.transpose(lse, (0, 1, 4, 2, 3)).reshape(n, nb * BLOCK, kvh, g)[:, :L]
    return o, lse


def dsa_project(h, w_in, pos):
    n, s, _ = h.shape
    q, kv, qi, ki, wi = split_last(h @ w_in, [A_HEADS * A_HEAD_DIM, 2 * A_KV_HEADS * A_HEAD_DIM,
                                             A_IDX_HEADS * A_IDX_DIM, A_IDX_DIM, A_IDX_HEADS])
    q = rope(q.reshape(n, s, A_HEADS, A_HEAD_DIM), pos)
    kv = kv.reshape(n, s, 2, A_KV_HEADS, A_HEAD_DIM)
    kv = jnp.stack([rope(kv[:, :, 0], pos), kv[:, :, 1]], axis=2)
    qi = rope(qi.reshape(n, s, A_IDX_HEADS, A_IDX_DIM), pos)
    ki = rope(ki[:, :, None, :], pos)[:, :, 0]
    return q, kv, qi, ki, wi


def index_scores(qi, wi, ki):
    r = jax.nn.relu(jnp.einsum('nthd,nsd->nths', qi, ki).astype(jnp.float32))
    return jnp.einsum('nths,nth->nts', r, wi.astype(jnp.float32)) * A_IDX_SCALE


def dsa_prompt(h, w_in, w_out):
    n, s, _ = h.shape
    q, kv, qi, ki, wi = dsa_project(h, w_in, jnp.arange(s))
    topk = min(A_TOPK, s // 4)
    key_pos = jnp.arange(s)

    def block_fn(args):
        qb, qib, wib, start = args
        sc = index_scores(qib, wib, ki)
        causal = key_pos[None, :] <= (start + jnp.arange(BLOCK))[:, None]
        vals, idx = lax.top_k(jnp.where(causal[None], sc, -jnp.inf), topk)
        kv_sel = jax.vmap(lambda a, i: a[i])(kv, idx)
        att = jnp.einsum('ntkgd,ntjkd->ntkgj', group_q(qb, A_KV_HEADS), kv_sel[:, :, :, 0])
        p, _ = masked_softmax(att.astype(jnp.float32) * A_HEAD_DIM ** -0.5,
                              jnp.isfinite(vals)[:, :, None, None, :])
        o = jnp.einsum('ntkgj,ntjkd->ntkgd', p.astype(kv.dtype), kv_sel[:, :, :, 1])
        return o.reshape(n, BLOCK, A_HEADS * A_HEAD_DIM)

    o = lax.map(block_fn, (to_query_blocks(q), to_query_blocks(qi), to_query_blocks(wi),
                           jnp.arange(s // BLOCK) * BLOCK))
    return from_query_blocks(o) @ w_out, kv, ki


def dsa_sample(h, kv_pool, idxk_pool, page_table, w_in, w_out):
    n, t, _ = h.shape
    past = page_table.shape[1] * PAGE_SIZE
    pos = past + jnp.arange(t)
    q, kv, qi, ki, wi = dsa_project(h, w_in, pos)
    total = past + t
    topk = min(A_TOPK, total // 4)
    ki_all = jnp.concatenate([idxk_pool[page_table].reshape(n, past, A_IDX_DIM), ki], axis=1)
    sc = index_scores(qi, wi, ki_all)
    causal = jnp.arange(total)[None, :] <= pos[:, None]
    vals, idx = lax.top_k(jnp.where(causal[None], sc, -jnp.inf), topk)
    valid = jnp.isfinite(vals)
    pidx = jnp.minimum(idx, past - 1)
    phys = page_table[jnp.arange(n)[:, None, None], pidx // PAGE_SIZE]
    kv_past = kv_pool[phys, pidx % PAGE_SIZE]
    in_past = valid & (idx < past)
    sel_new = jnp.any((idx[..., None] == past + jnp.arange(t)) & valid[..., None], axis=2)
    qg = group_q(q, A_KV_HEADS)
    att = jnp.concatenate([jnp.einsum('ntkgd,ntjkd->ntkgj', qg, kv_past[:, :, :, 0]),
                           jnp.einsum('ntkgd,nskd->ntkgs', qg, kv[:, :, 0])], axis=-1)
    mask = jnp.concatenate([in_past, sel_new], axis=-1)[:, :, None, None, :]
    p, _ = masked_softmax(att.astype(jnp.float32) * A_HEAD_DIM ** -0.5, mask)
    p = p.astype(kv.dtype)
    o = (jnp.einsum('ntkgj,ntjkd->ntkgd', p[..., :topk], kv_past[:, :, :, 1])
         + jnp.einsum('ntkgs,nskd->ntkgd', p[..., topk:], kv[:, :, 1]))
    return o.reshape(n, t, A_HEADS * A_HEAD_DIM) @ w_out, kv, ki


def swa_project(h, w_in, pos):
    n, s, _ = h.shape
    q, kv = split_last(h @ w_in, [B_HEADS * B_HEAD_DIM, 2 * B_KV_HEADS * B_HEAD_DIM])
    q = rope(q.reshape(n, s, B_HEADS, B_HEAD_DIM), pos)
    kv = kv.reshape(n, s, 2, B_KV_HEADS, B_HEAD_DIM)
    kv = jnp.stack([rope(kv[:, :, 0], pos), kv[:, :, 1]], axis=2)
    return group_q(q, B_KV_HEADS), kv


def swa_prompt(h, w_in, sinks, w_out):
    n, s, _ = h.shape
    qg, kv = swa_project(h, w_in, jnp.arange(s))
    o, _ = banded_window_attention(qg, kv[:, :, 0], kv[:, :, 1], B_WINDOW,
                                   sinks.reshape(B_KV_HEADS, -1))
    keep = min(B_WINDOW, s)
    return o.reshape(n, s, B_HEADS * B_HEAD_DIM) @ w_out, kv[:, s - keep:]


def swa_sample(h, buf, past, w_in, sinks, w_out):
    n, t, _ = h.shape
    wb = buf.shape[1]
    qg, kv = swa_project(h, w_in, past + jnp.arange(t))
    kv_all = jnp.concatenate([buf, kv], axis=1)
    att = jnp.einsum('ntkgd,nskd->nkgts', qg, kv_all[:, :, 0]).astype(jnp.float32) * B_HEAD_DIM ** -0.5
    dist = jnp.arange(t)[:, None] + wb - jnp.arange(wb + t)[None, :]
    sink = sinks.reshape(B_KV_HEADS, -1).astype(jnp.float32)[None, :, :, None, None]
    p, _ = masked_softmax(att, (dist >= 0) & (dist <= B_WINDOW), sink)
    o = jnp.einsum('nkgts,nskd->ntkgd', p.astype(kv.dtype), kv_all[:, :, 1])
    keep = min(B_WINDOW, past + t)
    return o.reshape(n, t, B_HEADS * B_HEAD_DIM) @ w_out, kv_all[:, wb + t - keep:]


def dilated_project(h, w_in, pos):
    n, s, _ = h.shape
    groups = []
    for gp in split_last(h @ w_in, [C_GROUP_IN] * len(C_GROUPS)):
        q, kv = split_last(gp, [C_HEADS * C_HEAD_DIM, 2 * C_KV_HEADS * C_HEAD_DIM])
        q = rope(q.reshape(n, s, C_HEADS, C_HEAD_DIM), pos)
        kv = kv.reshape(n, s, 2, C_KV_HEADS, C_HEAD_DIM)
        kv = jnp.stack([rope(kv[:, :, 0], pos), kv[:, :, 1]], axis=2)
        groups.append((group_q(q, C_KV_HEADS), kv))
    return groups


def merge_by_denominator(outs, lses):
    w = jax.nn.softmax(jnp.stack(lses, 0), axis=0)
    return jnp.sum(w[..., None] * jnp.stack(outs, 0).astype(jnp.float32), axis=0)


def dilated_prompt(h, w_in, w_out):
    n, s, _ = h.shape
    outs, lses, states = [], [], []
    for (window, dil), (qg, kv) in zip(C_GROUPS, dilated_project(h, w_in, jnp.arange(s))):
        def split_stride(a):
            a = jnp.moveaxis(a.reshape(n, s // dil, dil, *a.shape[2:]), 2, 1)
            return a.reshape(n * dil, s // dil, *a.shape[3:])

        def merge_stride(a):
            a = jnp.moveaxis(a.reshape(n, dil, s // dil, *a.shape[2:]), 1, 2)
            return a.reshape(n, s, *a.shape[3:])

        o, lse = banded_window_attention(split_stride(qg), split_stride(kv[:, :, 0]),
                                         split_stride(kv[:, :, 1]), window // dil)
        outs.append(merge_stride(o))
        lses.append(merge_stride(lse))
        states.append(kv[:, s - min(window, s):])
    o = merge_by_denominator(outs, lses).astype(h.dtype)
    return o.reshape(n, s, C_HEADS * C_HEAD_DIM) @ w_out, states


def dilated_sample(h, bufs, past, w_in, w_out):
    n, t, _ = h.shape
    outs, lses, states = [], [], []
    for (window, dil), (qg, kv), buf in zip(C_GROUPS, dilated_project(h, w_in, past + jnp.arange(t)), bufs):
        wb = buf.shape[1]
        kv_all = jnp.concatenate([buf, kv], axis=1)
        idx = wb + jnp.arange(t)[:, None] - dil * jnp.arange(window // dil + 1)[None, :]
        kv_sel = kv_all[:, jnp.maximum(idx, 0)]
        att = jnp.einsum('ntkgd,ntjkd->ntkgj', qg, kv_sel[:, :, :, 0]).astype(jnp.float32) * C_HEAD_DIM ** -0.5
        p, lse = masked_softmax(att, (idx >= 0)[None, :, None, None, :])
        outs.append(jnp.einsum('ntkgj,ntjkd->ntkgd', p.astype(kv.dtype), kv_sel[:, :, :, 1]))
        lses.append(lse)
        keep = min(window, past + t)
        states.append(kv_all[:, wb + t - keep:])
    o = merge_by_denominator(outs, lses).astype(h.dtype)
    return o.reshape(n, t, C_HEADS * C_HEAD_DIM) @ w_out, states


def mla_project(h, w_in, q_norm, w_q_up, kv_norm, pos):
    n, s, _ = h.shape
    c_q, c_kv, k_rope = split_last(h @ w_in, [D_Q_LORA, D_KV_LORA, D_ROPE])
    q = (rms_norm(c_q, q_norm) @ w_q_up).reshape(n, s, D_HEADS, D_NOPE + D_ROPE)
    return (q[..., :D_NOPE], rope(q[..., D_NOPE:], pos), rms_norm(c_kv, kv_norm),
            rope(k_rope[:, :, None, :], pos)[:, :, 0])


def mla_prompt(h, w_in, q_norm, w_q_up, kv_norm, w_kv_up, w_out):
    n, s, _ = h.shape
    q_nope, q_rope, c_kv, k_rope = mla_project(h, w_in, q_norm, w_q_up, kv_norm, jnp.arange(s))
    k_nope, v = split_last((c_kv @ w_kv_up).reshape(n, s, D_HEADS, D_NOPE + D_V), [D_NOPE, D_V])
    key_pos = jnp.arange(s)

    def block_fn(args):
        qn, qr, start = args
        sc = (jnp.einsum('nqhd,nshd->nhqs', qn, k_nope)
              + jnp.einsum('nqhr,nsr->nhqs', qr, k_rope)).astype(jnp.float32) * MLA_SCALE
        p, _ = masked_softmax(sc, key_pos[None, :] <= (start + jnp.arange(BLOCK))[:, None])
        return jnp.einsum('nhqs,nshd->nqhd', p.astype(v.dtype), v)

    o = lax.map(block_fn, (to_query_blocks(q_nope), to_query_blocks(q_rope), jnp.arange(s // BLOCK) * BLOCK))
    return from_query_blocks(o).reshape(n, s, D_HEADS * D_V) @ w_out, c_kv, k_rope


def mla_sample(h, lat_pool, rope_pool, page_table, w_in, q_norm, w_q_up, kv_norm, w_kv_up, w_out):
    n, t, _ = h.shape
    past = page_table.shape[1] * PAGE_SIZE
    q_nope, q_rope, c_kv, k_rope = mla_project(h, w_in, q_norm, w_q_up, kv_norm, past + jnp.arange(t))
    w_kv = w_kv_up.reshape(D_KV_LORA, D_HEADS, D_NOPE + D_V)
    w_uk, w_uv = w_kv[..., :D_NOPE], w_kv[..., D_NOPE:]
    q_lat = jnp.einsum('nthd,chd->nthc', q_nope, w_uk)

    def scores(c, kr):
        return (jnp.einsum('nthc,npc->nthp', q_lat, c)
                + jnp.einsum('nthr,npr->nthp', q_rope, kr)).astype(jnp.float32) * MLA_SCALE

    def online_update(carry, sc, c):
        m, l, acc = carry
        m_new = jnp.maximum(m, jnp.max(sc, axis=-1))
        corr = jnp.exp(m - m_new)
        e = jnp.exp(sc - m_new[..., None])
        return (m_new, l * corr + jnp.sum(e, axis=-1),
                acc * corr[..., None] + jnp.einsum('nthp,npc->nthc', e, c.astype(jnp.float32)))

    def page_step(carry, phys):
        c = lat_pool[phys]
        return online_update(carry, scores(c, rope_pool[phys]), c), None

    init = (jnp.full((n, t, D_HEADS), -jnp.inf, jnp.float32), jnp.zeros((n, t, D_HEADS), jnp.float32),
            jnp.zeros((n, t, D_HEADS, D_KV_LORA), jnp.float32))
    carry, _ = lax.scan(page_step, init, page_table.T)
    causal = jnp.arange(t)[None, :] <= jnp.arange(t)[:, None]
    s_new = jnp.where(causal[None, :, None, :], scores(c_kv, k_rope), -jnp.inf)
    _, l, acc = online_update(carry, s_new, c_kv)
    o_lat = (acc / l[..., None]).astype(h.dtype)
    o = jnp.einsum('nthc,chd->nthd', o_lat, w_uv).reshape(n, t, D_HEADS * D_V)
    return o @ w_out, c_kv, k_rope


def setup_inputs(seed: int = 0) -> dict:
    key = jax.random.key(seed)
    keys = iter(jax.random.split(key, 32))

    def normal(shape, scale=1.0):
        return scale * jax.random.normal(next(keys), shape, jnp.float32)

    def linear(fan_in, fan_out, lead=()):
        return normal((*lead, fan_in, fan_out), fan_in ** -0.5)

    def gain(shape):
        return 1.0 + normal(shape, 0.02)

    n_pages = PAST_LEN // PAGE_SIZE
    n_used = DEC_BATCH * n_pages
    n_pool = n_used + max(1, n_used // 4)
    page_table = jax.random.permutation(next(keys), n_pool)[:n_used].reshape(DEC_BATCH, n_pages).astype(jnp.int32)
    return {
        'x_prompt': normal((BATCH, SEQ, D_MODEL)),
        'x_sample': normal((DEC_BATCH, DEC_SEQ, D_MODEL)),
        'cache_a_kv': normal((n_pool, PAGE_SIZE, 2, A_KV_HEADS, A_HEAD_DIM)),
        'cache_a_idx_k': normal((n_pool, PAGE_SIZE, A_IDX_DIM)),
        'state_b_kv': normal((DEC_BATCH, min(B_WINDOW, PAST_LEN), 2, B_KV_HEADS, B_HEAD_DIM)),
        'state_c_kv_0': normal((DEC_BATCH, min(C_GROUPS[0][0], PAST_LEN), 2, C_KV_HEADS, C_HEAD_DIM)),
        'state_c_kv_1': normal((DEC_BATCH, min(C_GROUPS[1][0], PAST_LEN), 2, C_KV_HEADS, C_HEAD_DIM)),
        'state_c_kv_2': normal((DEC_BATCH, min(C_GROUPS[2][0], PAST_LEN), 2, C_KV_HEADS, C_HEAD_DIM)),
        'cache_d_latent': normal((n_pool, PAGE_SIZE, D_KV_LORA)),
        'cache_d_k_rope': normal((n_pool, PAGE_SIZE, D_ROPE)),
        'page_table': page_table,
        'norm_mix': gain((DEPTH, D_MODEL)),
        'norm_ffn': gain((DEPTH, D_MODEL)),
        'norm_final': gain((D_MODEL,)),
        'a_w_in': linear(D_MODEL, A_IN),
        'a_w_out': linear(A_HEADS * A_HEAD_DIM, D_MODEL),
        'b_w_in': linear(D_MODEL, B_IN),
        'b_sinks': normal((B_HEADS,)),
        'b_w_out': linear(B_HEADS * B_HEAD_DIM, D_MODEL),
        'c_w_in': linear(D_MODEL, C_IN),
        'c_w_out': linear(C_HEADS * C_HEAD_DIM, D_MODEL),
        'd_w_in': linear(D_MODEL, D_IN),
        'd_q_norm': gain((D_Q_LORA,)),
        'd_w_q_up': linear(D_Q_LORA, D_HEADS * (D_NOPE + D_ROPE)),
        'd_kv_norm': gain((D_KV_LORA,)),
        'd_w_kv_up': linear(D_KV_LORA, D_HEADS * (D_NOPE + D_V)),
        'd_w_out': linear(D_HEADS * D_V, D_MODEL),
        'ffn_w_in': linear(D_MODEL, 2 * D_FF, (DEPTH,)),
        'ffn_w_out': linear(D_FF, D_MODEL, (DEPTH,)),
    }


def reference(x_prompt, x_sample, cache_a_kv, cache_a_idx_k, state_b_kv, state_c_kv_0, state_c_kv_1,
              state_c_kv_2, cache_d_latent, cache_d_k_rope, page_table, norm_mix, norm_ffn, norm_final,
              a_w_in, a_w_out, b_w_in, b_sinks, b_w_out, c_w_in, c_w_out, d_w_in, d_q_norm, d_w_q_up,
              d_kv_norm, d_w_kv_up, d_w_out, ffn_w_in, ffn_w_out):
    past = page_table.shape[1] * PAGE_SIZE
    xp, xs = x_prompt, x_sample
    for layer in range(DEPTH):
        kind = layer % N_MIXERS
        hp = rms_norm(xp, norm_mix[layer])
        hs = rms_norm(xs, norm_mix[layer])
        if kind == 0:
            yp, a_kv_p, a_ik_p = dsa_prompt(hp, a_w_in, a_w_out)
            ys, a_kv_s, a_ik_s = dsa_sample(hs, cache_a_kv, cache_a_idx_k, page_table, a_w_in, a_w_out)
        elif kind == 1:
            yp, b_kv_p = swa_prompt(hp, b_w_in, b_sinks, b_w_out)
            ys, b_kv_s = swa_sample(hs, state_b_kv, past, b_w_in, b_sinks, b_w_out)
        elif kind == 2:
            yp, (c0_p, c1_p, c2_p) = dilated_prompt(hp, c_w_in, c_w_out)
            ys, (c0_s, c1_s, c2_s) = dilated_sample(hs, (state_c_kv_0, state_c_kv_1, state_c_kv_2), past,
                                                    c_w_in, c_w_out)
        else:
            yp, d_lat_p, d_kr_p = mla_prompt(hp, d_w_in, d_q_norm, d_w_q_up, d_kv_norm, d_w_kv_up, d_w_out)
            ys, d_lat_s, d_kr_s = mla_sample(hs, cache_d_latent, cache_d_k_rope, page_table, d_w_in, d_q_norm,
                                             d_w_q_up, d_kv_norm, d_w_kv_up, d_w_out)
        xp = xp + yp
        xs = xs + ys
        xp = xp + swiglu(rms_norm(xp, norm_ffn[layer]), ffn_w_in[layer], ffn_w_out[layer])
        xs = xs + swiglu(rms_norm(xs, norm_ffn[layer]), ffn_w_in[layer], ffn_w_out[layer])
    y_prompt = rms_norm(xp, norm_final)
    y_sample = rms_norm(xs, norm_final)
    return (y_prompt, y_sample, a_kv_p, a_kv_s, a_ik_p, a_ik_s, b_kv_p, b_kv_s, c0_p, c0_s, c1_p, c1_s,
            c2_p, c2_s, d_lat_p, d_lat_s, d_kr_p, d_kr_s)
```

```python
import functools
import math

import jax
import jax.numpy as jnp
from jax import lax
from jax.experimental import pallas as pl
from jax.experimental.pallas import tpu as pltpu

F32 = jnp.float32
BF16 = jnp.bfloat16
I32 = jnp.int32

LANES = 128
SUBLANES = 8
QBLK = 128
PAGE = 128
NORM_EPS = 1e-6
ROPE_THETA = 10000.0
NEG_BIG = -1e30
VMEM_LIMIT = 56 * 1024 * 1024

A_HEADS, A_HEAD_DIM, A_KV_HEADS, A_IDX_HEADS, A_IDX_DIM, A_TOPK = 16, 128, 4, 16, 128, 256
A_IDX_SCALE = (A_IDX_HEADS * A_IDX_DIM) ** -0.5
B_HEADS, B_HEAD_DIM, B_KV_HEADS, B_WINDOW = 32, 64, 4, 128
C_GROUPS = ((128, 1), (512, 4), (2048, 16))
C_HEADS, C_KV_HEADS, C_HEAD_DIM = 8, 2, 128
D_HEADS, D_NOPE, D_ROPE, D_V, D_Q_LORA, D_KV_LORA = 16, 128, 64, 128, 512, 512
MLA_SCALE = (D_NOPE + D_ROPE) ** -0.5


def _params(sem):
    return pltpu.CompilerParams(dimension_semantics=sem, vmem_limit_bytes=VMEM_LIMIT)


def _pick(total, pref):
    t = min(pref, total)
    while total % t:
        t //= 2
    return t


def _rope_tables(pos, dh):
    inv = jnp.power(ROPE_THETA, -jnp.arange(0, dh, 2, dtype=F32) / dh)
    ang = pos.astype(F32)[:, None] * inv[None, :]
    cos, sin = jnp.cos(ang), jnp.sin(ang)
    c = jnp.concatenate([cos, cos], axis=-1)
    s = jnp.concatenate([-sin, sin], axis=-1)
    rep = LANES // dh
    return jnp.tile(c, (1, rep)), jnp.tile(s, (1, rep))


def _rope_block(x, cos, sin, dh):
    half = dh // 2
    if dh == LANES:
        rot = pltpu.roll(x, half, axis=1)
    else:
        lane = lax.broadcasted_iota(I32, x.shape, 1)
        first = (lane % dh) < half
        rot = jnp.where(first, pltpu.roll(x, LANES - half, axis=1), pltpu.roll(x, half, axis=1))
    return x * cos + rot * sin


def _norm_proj_kernel(flags_ref, x_ref, g_ref, w_ref, cos_ref, sin_ref, o32_ref, o16_ref, xn_ref,
                      *, dh, nsub):
    j = pl.program_id(1)

    @pl.when(j == 0)
    def _():
        xf = x_ref[...]
        y = xf * lax.rsqrt(jnp.mean(xf * xf, axis=-1, keepdims=True) + NORM_EPS)
        xn_ref[...] = (y * g_ref[...]).astype(BF16)

    acc = jnp.dot(xn_ref[...], w_ref[...], preferred_element_type=F32)
    for u in range(nsub):
        blk = acc[:, u * LANES:(u + 1) * LANES]
        roped = _rope_block(blk, cos_ref[...], sin_ref[...], dh)
        out = jnp.where(flags_ref[j * nsub + u] > 0, roped, blk)
        o32_ref[:, u * LANES:(u + 1) * LANES] = out
        o16_ref[:, u * LANES:(u + 1) * LANES] = out.astype(BF16)


def norm_proj(x, xcb, g, w, flags, cos, sin, dh, rows=None, tm=512, tn=512):
    m = x.shape[0] if rows is None else rows
    k, n = w.shape
    tm = _pick(m, tm)
    tn = _pick(n, tn)
    nsub = tn // LANES
    grid_spec = pltpu.PrefetchScalarGridSpec(
        num_scalar_prefetch=1,
        grid=(m // tm, n // tn),
        in_specs=[
            pl.BlockSpec((tm, k), lambda i, j, f: (i, xcb)),
            pl.BlockSpec((1, k), lambda i, j, f: (0, 0)),
            pl.BlockSpec((k, tn), lambda i, j, f: (0, j)),
            pl.BlockSpec((tm, LANES), lambda i, j, f: (i, 0)),
            pl.BlockSpec((tm, LANES), lambda i, j, f: (i, 0)),
        ],
        out_specs=[
            pl.BlockSpec((tm, tn), lambda i, j, f: (i, j)),
            pl.BlockSpec((tm, tn), lambda i, j, f: (i, j)),
        ],
        scratch_shapes=[pltpu.VMEM((tm, k), BF16)],
    )
    return pl.pallas_call(
        functools.partial(_norm_proj_kernel, dh=dh, nsub=nsub),
        grid_spec=grid_spec,
        out_shape=[jax.ShapeDtypeStruct((m, n), F32), jax.ShapeDtypeStruct((m, n), BF16)],
        compiler_params=_params(("parallel", "arbitrary")),
        name="norm_proj",
    )(flags, x, g, w, cos, sin)


def _rmsnorm_kernel(x_ref, g_ref, o_ref):
    xf = x_ref[...]
    y = xf * lax.rsqrt(jnp.mean(xf * xf, axis=-1, keepdims=True) + NORM_EPS)
    o_ref[...] = y * g_ref[...]


def rmsnorm(x, xcb, g, tm=512):
    m = x.shape[0]
    k = g.shape[1]
    tm = _pick(m, tm)
    return pl.pallas_call(
        _rmsnorm_kernel,
        grid=(m // tm,),
        in_specs=[pl.BlockSpec((tm, k), lambda i: (i, xcb)), pl.BlockSpec((1, k), lambda i: (0, 0))],
        out_specs=pl.BlockSpec((tm, k), lambda i: (i, 0)),
        out_shape=jax.ShapeDtypeStruct((m, k), F32),
        compiler_params=_params(("parallel",)),
        name="rmsnorm",
    )(x, g)


def _mm_res_kernel(a_ref, w_ref, r_ref, o_ref):
    o_ref[...] = r_ref[...] + jnp.dot(a_ref[...].astype(BF16), w_ref[...], preferred_element_type=F32)


def mm_res(a, w, res, tm=512, tn=512):
    m, k = a.shape
    n = w.shape[1]
    tm = _pick(m, tm)
    tn = _pick(n, tn)
    return pl.pallas_call(
        _mm_res_kernel,
        grid=(m // tm, n // tn),
        in_specs=[
            pl.BlockSpec((tm, k), lambda i, j: (i, 0)),
            pl.BlockSpec((k, tn), lambda i, j: (0, j)),
            pl.BlockSpec((tm, tn), lambda i, j: (i, j)),
        ],
        out_specs=pl.BlockSpec((tm, tn), lambda i, j: (i, j)),
        out_shape=jax.ShapeDtypeStruct((m, n), F32),
        compiler_params=_params(("parallel", "parallel")),
        name="mm_res",
    )(a, w, res)


def _merge_mm_res_kernel(o0_ref, o1_ref, o2_ref, l0_ref, l1_ref, l2_ref, w_ref, r_ref, out_ref, mg_ref):
    j = pl.program_id(1)

    @pl.when(j == 0)
    def _():
        l0, l1, l2 = l0_ref[...], l1_ref[...], l2_ref[...]
        mx = jnp.maximum(jnp.maximum(l0, l1), l2)
        e0, e1, e2 = jnp.exp(l0 - mx), jnp.exp(l1 - mx), jnp.exp(l2 - mx)
        den = e0 + e1 + e2
        mg = (e0 / den) * o0_ref[...] + (e1 / den) * o1_ref[...] + (e2 / den) * o2_ref[...]
        mg_ref[...] = mg.astype(BF16)

    out_ref[...] = r_ref[...] + jnp.dot(mg_ref[...], w_ref[...], preferred_element_type=F32)


def merge_mm_res(outs, lses, w, res, tm=256, tn=512):
    m, k = outs[0].shape
    n = w.shape[1]
    tm = _pick(m, tm)
    tn = _pick(n, tn)
    row = pl.BlockSpec((tm, k), lambda i, j: (i, 0))
    return pl.pallas_call(
        _merge_mm_res_kernel,
        grid=(m // tm, n // tn),
        in_specs=[row] * 6 + [
            pl.BlockSpec((k, tn), lambda i, j: (0, j)),
            pl.BlockSpec((tm, tn), lambda i, j: (i, j)),
        ],
        out_specs=pl.BlockSpec((tm, tn), lambda i, j: (i, j)),
        out_shape=jax.ShapeDtypeStruct((m, n), F32),
        scratch_shapes=[pltpu.VMEM((tm, k), BF16)],
        compiler_params=_params(("parallel", "arbitrary")),
        name="merge_mm_res",
    )(*outs, *lses, w, res)


def _ffn_kernel(x_ref, g_ref, wg_ref, wu_ref, wo_ref, o_ref, xn_ref, acc_ref):
    f = pl.program_id(1)

    @pl.when(f == 0)
    def _():
        xf = x_ref[...]
        y = xf * lax.rsqrt(jnp.mean(xf * xf, axis=-1, keepdims=True) + NORM_EPS)
        xn_ref[...] = (y * g_ref[...]).astype(BF16)
        acc_ref[...] = xf

    xn = xn_ref[...]
    gate = jnp.dot(xn, wg_ref[...], preferred_element_type=F32)
    up = jnp.dot(xn, wu_ref[...], preferred_element_type=F32)
    h = (gate * jax.nn.sigmoid(gate) * up).astype(BF16)
    acc_ref[...] += jnp.dot(h, wo_ref[...], preferred_element_type=F32)

    @pl.when(f == pl.num_programs(1) - 1)
    def _():
        o_ref[...] = acc_ref[...]


def ffn(x, g, w_in, w_out, tm=512, tf=512):
    m, d = x.shape
    ff = w_out.shape[0]
    tm = _pick(m, tm)
    tf = _pick(ff, tf)
    nf = ff // tf
    return pl.pallas_call(
        _ffn_kernel,
        grid=(m // tm, nf),
        in_specs=[
            pl.BlockSpec((tm, d), lambda i, f: (i, 0)),
            pl.BlockSpec((1, d), lambda i, f: (0, 0)),
            pl.BlockSpec((d, tf), lambda i, f: (0, f)),
            pl.BlockSpec((d, tf), lambda i, f: (0, f + nf)),
            pl.BlockSpec((tf, d), lambda i, f: (f, 0)),
        ],
        out_specs=pl.BlockSpec((tm, d), lambda i, f: (i, 0)),
        out_shape=jax.ShapeDtypeStruct((m, d), F32),
        scratch_shapes=[pltpu.VMEM((tm, d), BF16), pltpu.VMEM((tm, d), F32)],
        compiler_params=_params(("parallel", "arbitrary")),
        name="ffn",
    )(x, g, w_in, w_in, w_out)


_NT = (((1,), (1,)), ((), ()))


def _band_attn_kernel(sink_ref, q_ref, k_ref, v_ref, *rest, G, Dh, Dr, Dv, nheads, ck, window, scale,
                      has_sink, want_lse, heads_in_grid):
    if Dr:
        k2_ref, rest = rest[0], rest[1:]
    o_ref = rest[0]
    lse_ref = rest[1] if want_lse else None
    qb = pl.program_id(3 if heads_in_grid else 2)
    head0 = pl.program_id(2) if heads_in_grid else 0
    dq = Dh + Dr
    rows = G * QBLK
    q0 = qb * QBLK
    c_hi = (q0 + QBLK - 1) // ck
    c_lo = 0 if window is None else jnp.maximum(q0 - window, 0) // ck
    row = lax.broadcasted_iota(I32, (QBLK, ck), 0)
    col = lax.broadcasted_iota(I32, (QBLK, ck), 1)

    for hh in range(nheads):
        q_all = q_ref[:, hh * G * dq:(hh + 1) * G * dq]
        qs = jnp.concatenate([q_all[:, g * dq:g * dq + Dh] for g in range(G)], axis=0)
        if Dr:
            q2 = jnp.concatenate([q_all[:, g * dq + Dh:(g + 1) * dq] for g in range(G)], axis=0)
        if has_sink:
            m0 = jnp.concatenate(
                [jnp.full((QBLK, 1), sink_ref[(head0 + hh) * G + g], F32) for g in range(G)], axis=0)
            l0 = jnp.ones((rows, 1), F32)
        else:
            m0 = jnp.full((rows, 1), NEG_BIG, F32)
            l0 = jnp.zeros((rows, 1), F32)

        def step(c, carry, hh=hh, qs=qs, q2=q2 if Dr else None):
            m, l, acc = carry
            off = pl.multiple_of(c * ck, ck)
            kc = k_ref[pl.ds(off, ck), hh * Dh:(hh + 1) * Dh]
            s = lax.dot_general(qs, kc, _NT, preferred_element_type=F32)
            if Dr:
                s = s + lax.dot_general(q2, k2_ref[pl.ds(off, ck), :], _NT, preferred_element_type=F32)
            s = s * scale
            dist = (q0 + row) - (off + col)
            valid = dist >= 0
            if window is not None:
                valid = valid & (dist <= window)
            s = jnp.where(valid[None], s.reshape(G, QBLK, ck), -jnp.inf).reshape(rows, ck)
            m_new = jnp.maximum(m, jnp.max(s, axis=1, keepdims=True))
            alpha = jnp.exp(m - m_new)
            p = jnp.exp(s - m_new)
            l = alpha * l + jnp.sum(p, axis=1, keepdims=True)
            vc = v_ref[pl.ds(off, ck), hh * Dv:(hh + 1) * Dv]
            acc = alpha * acc + jnp.dot(p.astype(BF16), vc, preferred_element_type=F32)
            return m_new, l, acc

        m, l, acc = lax.fori_loop(c_lo, c_hi + 1, step, (m0, l0, jnp.zeros((rows, Dv), F32)))
        o = acc / l
        for g in range(G):
            lo = (hh * G + g) * Dv
            o_ref[:, lo:lo + Dv] = o[g * QBLK:(g + 1) * QBLK].astype(o_ref.dtype)
        if want_lse:
            lse = m + jnp.log(l)
            for g in range(G):
                lo = (hh * G + g) * Dv
                lse_ref[:, lo:lo + Dv] = jnp.broadcast_to(lse[g * QBLK:(g + 1) * QBLK], (QBLK, Dv))


def band_attn(qarr, karr, varr, *, nb, s, d, Hkv, G, Dh, Dv, q_cb, k_cb, v_cb, heads_in_grid, ck, window,
              scale, sinks=None, k2arr=None, k2_cb=None, Dr=0, want_lse=False, out_dtype=BF16):
    s_sub = s // d
    nqb = s_sub // QBLK
    nheads = 1 if heads_in_grid else Hkv
    hq = Hkv * G
    dq = Dh + Dr

    def view(a):
        return a.reshape(a.shape[0] // d, d * a.shape[1])

    if heads_in_grid:
        grid = (nb, d, Hkv, nqb)
        qmap = lambda b, r, h, i, sk: (b * nqb + i, q_cb(r, h))
        kmap = lambda b, r, h, i, sk: (b, k_cb(r, h))
        vmap = lambda b, r, h, i, sk: (b, v_cb(r, h))
        k2map = (lambda b, r, h, i, sk: (b, k2_cb(r, h))) if Dr else None
        omap = lambda b, r, h, i, sk: (b * nqb + i, r * Hkv + h)
        sem = ("parallel", "parallel", "parallel", "parallel")
    else:
        grid = (nb, d, nqb)
        qmap = lambda b, r, i, sk: (b * nqb + i, q_cb(r, 0))
        kmap = lambda b, r, i, sk: (b, k_cb(r, 0))
        vmap = lambda b, r, i, sk: (b, v_cb(r, 0))
        k2map = (lambda b, r, i, sk: (b, k2_cb(r, 0))) if Dr else None
        omap = lambda b, r, i, sk: (b * nqb + i, r)
        sem = ("parallel", "parallel", "parallel")

    in_specs = [
        pl.BlockSpec((QBLK, nheads * G * dq), qmap),
        pl.BlockSpec((s_sub, nheads * Dh), kmap),
        pl.BlockSpec((s_sub, nheads * Dv), vmap),
    ]
    args = [view(qarr), view(karr), view(varr)]
    if Dr:
        in_specs.append(pl.BlockSpec((s_sub, Dr), k2map))
        args.append(view(k2arr))
    oshape = jax.ShapeDtypeStruct((nb * s_sub, d * hq * Dv), out_dtype)
    out_specs = [pl.BlockSpec((QBLK, nheads * G * Dv), omap)]
    out_shape = [oshape]
    if want_lse:
        out_specs.append(pl.BlockSpec((QBLK, nheads * G * Dv), omap))
        out_shape.append(jax.ShapeDtypeStruct((nb * s_sub, d * hq * Dv), F32))
    has_sink = sinks is not None
    if sinks is None:
        sinks = jnp.zeros((hq,), F32)
    res = pl.pallas_call(
        functools.partial(_band_attn_kernel, G=G, Dh=Dh, Dr=Dr, Dv=Dv, nheads=nheads, ck=ck, window=window,
                          scale=scale, has_sink=has_sink, want_lse=want_lse, heads_in_grid=heads_in_grid),
        grid_spec=pltpu.PrefetchScalarGridSpec(
            num_scalar_prefetch=1, grid=grid, in_specs=in_specs, out_specs=out_specs),
        out_shape=out_shape,
        compiler_params=_params(sem),
        name="band_attn",
    )(sinks.astype(F32), *args)
    res = [a.reshape(nb * s, hq * Dv) for a in res]
    return res if want_lse else res[0]


INT_MIN = -2 ** 31


def _float_key(x):
    i = pltpu.bitcast(x + 0.0, I32)
    return i ^ ((i >> 31) & 0x7FFFFFFF)


def _kth_key(count_ge, rows, k):
    def body(it, t):
        bit = 31 - it
        cand = t + lax.shift_left(jnp.int32(1), bit)
        return jnp.where(count_ge(cand) >= k, cand, t)

    return lax.fori_loop(0, 32, body, jnp.full((rows, 1), INT_MIN, I32))


def _dsa_prompt_kernel(q_ref, qi_ref, wi_ref, ki_ref, k_ref, v_ref, o_ref, key_ref, *, ck, topk, scale):
    qb = pl.program_id(1)
    q0 = qb * QBLK
    nk = (q0 + QBLK - 1) // ck + 1
    row = lax.broadcasted_iota(I32, (QBLK, ck), 0)
    col = lax.broadcasted_iota(I32, (QBLK, ck), 1)
    w = wi_ref[...]
    g = A_HEADS // A_KV_HEADS
    dh = A_HEAD_DIM

    def score_chunk(c, carry):
        off = pl.multiple_of(c * ck, ck)
        kic = ki_ref[pl.ds(off, ck), :]
        acc = jnp.zeros((QBLK, ck), F32)
        for h in range(A_IDX_HEADS):
            r = lax.dot_general(qi_ref[:, h * A_IDX_DIM:(h + 1) * A_IDX_DIM], kic, _NT,
                                preferred_element_type=F32)
            acc = acc + jnp.maximum(r, 0.0) * w[:, h:h + 1]
        sc = jnp.where((off + col) <= (q0 + row), acc * A_IDX_SCALE, -jnp.inf)
        key_ref[:, pl.ds(off, ck)] = _float_key(sc)
        return carry

    lax.fori_loop(0, nk, score_chunk, 0)

    def count_ge(cand):
        def inner(c, cnt):
            off = pl.multiple_of(c * ck, ck)
            ge = (key_ref[:, pl.ds(off, ck)] >= cand).astype(I32)
            for u in range(ck // LANES):
                cnt = cnt + ge[:, u * LANES:(u + 1) * LANES]
            return cnt
        cnt = lax.fori_loop(0, nk, inner, jnp.zeros((QBLK, LANES), I32))
        return jnp.sum(cnt, axis=1, keepdims=True)

    thr = _kth_key(count_ge, QBLK, topk)
    rows = g * QBLK

    for hh in range(A_KV_HEADS):
        qs = jnp.concatenate([q_ref[:, (hh * g + j) * dh:(hh * g + j + 1) * dh] for j in range(g)], axis=0)

        def step(c, carry, hh=hh, qs=qs):
            m, l, acc = carry
            off = pl.multiple_of(c * ck, ck)
            s = lax.dot_general(qs, k_ref[pl.ds(off, ck), hh * dh:(hh + 1) * dh], _NT,
                                preferred_element_type=F32) * scale
            sel = (key_ref[:, pl.ds(off, ck)] >= thr) & ((off + col) <= (q0 + row))
            s = jnp.where(sel[None], s.reshape(g, QBLK, ck), -jnp.inf).reshape(rows, ck)
            m_new = jnp.maximum(m, jnp.max(s, axis=1, keepdims=True))
            alpha = jnp.exp(m - m_new)
            p = jnp.exp(s - m_new)
            l = alpha * l + jnp.sum(p, axis=1, keepdims=True)
            acc = alpha * acc + jnp.dot(p.astype(BF16), v_ref[pl.ds(off, ck), hh * dh:(hh + 1) * dh],
                                        preferred_element_type=F32)
            return m_new, l, acc

        m, l, acc = lax.fori_loop(0, nk, step, (jnp.full((rows, 1), NEG_BIG, F32),
                                                jnp.zeros((rows, 1), F32), jnp.zeros((rows, dh), F32)))
        o = acc / l
        for j in range(g):
            lo = (hh * g + j) * dh
            o_ref[:, lo:lo + dh] = o[j * QBLK:(j + 1) * QBLK].astype(o_ref.dtype)


A_Q0, A_QI0, A_K0, A_V0, A_KI0, A_WI0, A_NP = 0, 2048, 4096, 4608, 5120, 5248, 5376


def dsa_prompt(p16, p32, nb, s):
    nqb = s // QBLK
    ck = min(512, s)
    topk = min(A_TOPK, s // 4)
    hd = A_HEADS * A_HEAD_DIM
    kvd = A_KV_HEADS * A_HEAD_DIM
    return pl.pallas_call(
        functools.partial(_dsa_prompt_kernel, ck=ck, topk=topk, scale=A_HEAD_DIM ** -0.5),
        grid=(nb, nqb),
        in_specs=[
            pl.BlockSpec((QBLK, hd), lambda b, i: (b * nqb + i, A_Q0 // hd)),
            pl.BlockSpec((QBLK, hd), lambda b, i: (b * nqb + i, A_QI0 // hd)),
            pl.BlockSpec((QBLK, LANES), lambda b, i: (b * nqb + i, A_WI0 // LANES)),
            pl.BlockSpec((s, A_IDX_DIM), lambda b, i: (b, A_KI0 // A_IDX_DIM)),
            pl.BlockSpec((s, kvd), lambda b, i: (b, A_K0 // kvd)),
            pl.BlockSpec((s, kvd), lambda b, i: (b, A_V0 // kvd)),
        ],
        out_specs=pl.BlockSpec((QBLK, hd), lambda b, i: (b * nqb + i, 0)),
        out_shape=jax.ShapeDtypeStruct((nb * s, hd), BF16),
        scratch_shapes=[pltpu.VMEM((QBLK, s), I32)],
        compiler_params=_params(("parallel", "arbitrary")),
        name="dsa_prompt",
    )(p16, p16, p32, p16, p16, p16)


def _pad_rows(a, rows):
    return jnp.concatenate([a, jnp.zeros((rows - a.shape[0], a.shape[1]), a.dtype)], axis=0)


def _win_sample_kernel(sink_ref, q_ref, kb_ref, vb_ref, kn_ref, vn_ref, o_ref, lse_ref, *, sb, G, T, wb, window,
                       dil, scale, has_sink):
    h = pl.program_id(1)
    rows = G * T
    ncol = wb + LANES
    col = lax.broadcasted_iota(I32, (T, ncol), 1)
    t = lax.broadcasted_iota(I32, (T, ncol), 0)
    dist = wb + t - col
    valid = (dist >= 0) & (dist <= window) & ((dist & (dil - 1)) == 0)
    if has_sink:
        sink = jnp.concatenate([jnp.full((T, 1), sink_ref[h * G + g], F32) for g in range(G)], axis=0)
    for i in range(sb):
        q = q_ref[i]
        kb = kb_ref[i].astype(BF16)
        vb = vb_ref[i].astype(BF16)
        kn = _pad_rows(kn_ref[i], LANES).astype(BF16)
        vn = _pad_rows(vn_ref[i], LANES).astype(BF16)
        s = jnp.concatenate([lax.dot_general(q, kb, _NT, preferred_element_type=F32),
                             lax.dot_general(q, kn, _NT, preferred_element_type=F32)], axis=1) * scale
        s = jnp.where(valid[None], s.reshape(G, T, ncol), -jnp.inf).reshape(rows, ncol)
        m = jnp.max(s, axis=1, keepdims=True)
        if has_sink:
            m = jnp.maximum(m, sink)
        e = jnp.exp(s - m)
        den = jnp.sum(e, axis=1, keepdims=True)
        if has_sink:
            den = den + jnp.exp(sink - m)
        eb = e.astype(BF16)
        acc = (jnp.dot(eb[:, :wb], vb, preferred_element_type=F32)
               + jnp.dot(eb[:, wb:], vn, preferred_element_type=F32))
        o_ref[i] = acc / den
        lse_ref[i] = jnp.broadcast_to(m + jnp.log(den), (rows, o_ref.shape[-1]))


def win_sample_attn(q, kbuf, vbuf, kmap, vmap, kblock, knew, vnew, *, Hkv, G, T, Dh, wb, window, dil, scale,
                    sinks=None):
    nb = q.shape[0]
    sb = kblock[0]
    has_sink = sinks is not None
    if sinks is None:
        sinks = jnp.zeros((Hkv * G,), F32)
    qspec = pl.BlockSpec((sb, None, G * T, Dh), lambda b, h, sk: (b, h, 0, 0))
    nspec = pl.BlockSpec((sb, None, T, Dh), lambda b, h, sk: (b, h, 0, 0))
    ospec = pl.BlockSpec((sb, None, G * T, Dh), lambda b, h, sk: (b, h, 0, 0))
    oshape = jax.ShapeDtypeStruct((nb, Hkv, G * T, Dh), F32)
    return pl.pallas_call(
        functools.partial(_win_sample_kernel, sb=sb, G=G, T=T, wb=wb, window=window, dil=dil, scale=scale,
                          has_sink=has_sink),
        grid_spec=pltpu.PrefetchScalarGridSpec(
            num_scalar_prefetch=1, grid=(nb // sb, Hkv),
            in_specs=[qspec, pl.BlockSpec(kblock, kmap), pl.BlockSpec(kblock, vmap), nspec, nspec],
            out_specs=[ospec, ospec]),
        out_shape=[oshape, oshape],
        compiler_params=_params(("parallel", "parallel")),
        name="win_sample_attn",
    )(sinks.astype(F32), q, kbuf, vbuf, knew, vnew)


def _page_specs(shape_tail, npg, upg):
    nd = len(shape_tail)
    return [pl.BlockSpec((None, PAGE) + shape_tail,
                         (lambda b, p, pt, u=u: (pt[b * npg + p * upg + u],) + (0,) * (nd + 1)))
            for u in range(upg)]


def _index_rows(qi, w, kic, T):
    r = lax.dot_general(qi, kic, _NT, preferred_element_type=F32)
    x = jnp.maximum(r, 0.0) * w[:, :1]
    return jnp.sum(x.reshape(A_IDX_HEADS, T, x.shape[1]), axis=0) * A_IDX_SCALE


def _dsa_scores_kernel(pt_ref, qi_ref, w_ref, *rest, upg, T):
    pages, o_ref = rest[:upg], rest[upg]
    for u in range(upg):
        o_ref[:, u * PAGE:(u + 1) * PAGE] = _index_rows(qi_ref[...], w_ref[...], pages[u][...].astype(BF16), T)


def dsa_sample_scores(pt, qi_s, w_s, idxk_pool, *, npg, upg, T):
    nb = qi_s.shape[0]
    rows = A_IDX_HEADS * T
    return pl.pallas_call(
        functools.partial(_dsa_scores_kernel, upg=upg, T=T),
        grid_spec=pltpu.PrefetchScalarGridSpec(
            num_scalar_prefetch=1, grid=(nb, npg // upg),
            in_specs=[pl.BlockSpec((None, rows, A_IDX_DIM), lambda b, p, pt: (b, 0, 0)),
                      pl.BlockSpec((None, rows, LANES), lambda b, p, pt: (b, 0, 0))]
                     + _page_specs((A_IDX_DIM,), npg, upg),
            out_specs=pl.BlockSpec((None, T, upg * PAGE), lambda b, p, pt: (b, 0, p))),
        out_shape=jax.ShapeDtypeStruct((nb, T, npg * PAGE), F32),
        compiler_params=_params(("parallel", "parallel")),
        name="dsa_sample_scores",
    )(pt, qi_s, w_s, *([idxk_pool] * upg))


def _dsa_thr_kernel(sc_ref, qi_ref, w_ref, kin_ref, thr_ref, sel_ref, key_ref, *, sb, T, past, topk, ck):
    rows = sb * T
    key_ref[:, :past] = _float_key(sc_ref[...].reshape(rows, past))
    jn = lax.broadcasted_iota(I32, (T, LANES), 1)
    tn = lax.broadcasted_iota(I32, (T, LANES), 0)
    causal = (jn <= tn) & (jn < T)
    for i in range(sb):
        sn = _index_rows(qi_ref[i], w_ref[i], kin_ref[i], T)
        key_ref[i * T:(i + 1) * T, past:] = _float_key(jnp.where(causal, sn, -jnp.inf))

    def count_ge(cand):
        def inner(c, cnt):
            off = pl.multiple_of(c * ck, ck)
            ge = (key_ref[:, pl.ds(off, ck)] >= cand).astype(I32)
            for u in range(ck // LANES):
                cnt = cnt + ge[:, u * LANES:(u + 1) * LANES]
            return cnt
        cnt = lax.fori_loop(0, past // ck, inner, (key_ref[:, past:] >= cand).astype(I32))
        return jnp.sum(cnt, axis=1, keepdims=True)

    thr = _kth_key(count_ge, rows, topk)
    thr_ref[...] = jnp.broadcast_to(thr, (rows, LANES)).reshape(sb, T, LANES)
    sel = (key_ref[:, past:] >= thr).reshape(sb, T, LANES) & causal[None]
    sel_ref[...] = sel.astype(I32)


def dsa_sample_threshold(scores, qi_s, w_s, ki_new, *, T, sb=16):
    nb, _, past = scores.shape
    sb = _pick(nb, sb)
    rows = A_IDX_HEADS * T
    topk = min(A_TOPK, (past + T) // 4)
    ck = 512
    assert past % ck == 0
    shp = jax.ShapeDtypeStruct((nb, T, LANES), I32)
    return pl.pallas_call(
        functools.partial(_dsa_thr_kernel, sb=sb, T=T, past=past, topk=topk, ck=ck),
        grid=(nb // sb,),
        in_specs=[pl.BlockSpec((sb, T, past), lambda b: (b, 0, 0)),
                  pl.BlockSpec((sb, rows, A_IDX_DIM), lambda b: (b, 0, 0)),
                  pl.BlockSpec((sb, rows, LANES), lambda b: (b, 0, 0)),
                  pl.BlockSpec((sb, LANES, A_IDX_DIM), lambda b: (b, 0, 0))],
        out_specs=[pl.BlockSpec((sb, T, LANES), lambda b: (b, 0, 0))] * 2,
        out_shape=[shp, shp],
        scratch_shapes=[pltpu.VMEM((sb * T, past + LANES), I32)],
        compiler_params=_params(("parallel",)),
        name="dsa_sample_threshold",
    )(scores, qi_s, w_s, ki_new)


def _online_update(m_ref, l_ref, acc_ref, s, pv):
    m = m_ref[...]
    m_new = jnp.maximum(m, jnp.max(s, axis=1, keepdims=True))
    alpha = jnp.exp(m - m_new)
    p = jnp.exp(s - m_new)
    l_ref[...] = alpha * l_ref[...] + jnp.sum(p, axis=1, keepdims=True)
    acc_ref[...] = alpha * acc_ref[...] + pv(p.astype(BF16))
    m_ref[...] = m_new


def _dsa_attn_kernel(pt_ref, q_ref, sc_ref, thr_ref, sel_ref, kn_ref, vn_ref, *rest, upg, T, scale):
    pages = rest[:upg]
    o_ref, qbd_ref, m_ref, l_ref, acc_ref = rest[upg:]
    p = pl.program_id(1)
    kvd = A_KV_HEADS * A_HEAD_DIM
    rows = A_HEADS * T
    rper = rows // A_KV_HEADS

    @pl.when(p == 0)
    def _():
        q = q_ref[...]
        rr = lax.broadcasted_iota(I32, q.shape, 0) // rper
        for hh in range(A_KV_HEADS):
            qbd_ref[:, hh * A_HEAD_DIM:(hh + 1) * A_HEAD_DIM] = jnp.where(rr == hh, q, jnp.zeros_like(q))
        m_ref[...] = jnp.full(m_ref.shape, NEG_BIG, F32)
        l_ref[...] = jnp.zeros(l_ref.shape, F32)
        acc_ref[...] = jnp.zeros(acc_ref.shape, F32)

    qbd = qbd_ref[...]
    kv = [pg[...].astype(BF16) for pg in pages]
    s = jnp.concatenate([lax.dot_general(qbd, x[:, :kvd], _NT, preferred_element_type=F32) for x in kv],
                        axis=1) * scale
    thr = jnp.concatenate([thr_ref[...]] * upg, axis=1)
    sel = _float_key(sc_ref[...]) >= thr
    s = jnp.where(sel[None], s.reshape(A_HEADS, T, upg * PAGE), -jnp.inf).reshape(rows, upg * PAGE)

    def pv(pb):
        out = jnp.dot(pb[:, :PAGE], kv[0][:, kvd:], preferred_element_type=F32)
        for u in range(1, upg):
            out = out + jnp.dot(pb[:, u * PAGE:(u + 1) * PAGE], kv[u][:, kvd:], preferred_element_type=F32)
        return out

    _online_update(m_ref, l_ref, acc_ref, s, pv)

    @pl.when(p == pl.num_programs(1) - 1)
    def _():
        sn = lax.dot_general(qbd, kn_ref[...], _NT, preferred_element_type=F32) * scale
        sn = jnp.where((sel_ref[...] > 0)[None], sn.reshape(A_HEADS, T, LANES), -jnp.inf).reshape(rows, LANES)
        _online_update(m_ref, l_ref, acc_ref, sn,
                       lambda pb: jnp.dot(pb, vn_ref[...], preferred_element_type=F32))
        o = acc_ref[...] / l_ref[...]
        for hh in range(A_KV_HEADS):
            o_ref[hh * rper:(hh + 1) * rper, :] = (
                o[hh * rper:(hh + 1) * rper, hh * A_HEAD_DIM:(hh + 1) * A_HEAD_DIM].astype(o_ref.dtype))


def dsa_sample_attn(pt, q_s, scores, thr, sel_new, k_new, v_new, kv_pool, *, npg, upg, T):
    nb = q_s.shape[0]
    rows = A_HEADS * T
    kvd = A_KV_HEADS * A_HEAD_DIM
    per_seq = lambda shape: pl.BlockSpec((None,) + shape, lambda b, p, pt: (b, 0, 0))
    return pl.pallas_call(
        functools.partial(_dsa_attn_kernel, upg=upg, T=T, scale=A_HEAD_DIM ** -0.5),
        grid_spec=pltpu.PrefetchScalarGridSpec(
            num_scalar_prefetch=1, grid=(nb, npg // upg),
            in_specs=[per_seq((rows, A_HEAD_DIM)),
                      pl.BlockSpec((None, T, upg * PAGE), lambda b, p, pt: (b, 0, p)),
                      per_seq((T, LANES)), per_seq((T, LANES)),
                      per_seq((LANES, kvd)), per_seq((LANES, kvd))]
                     + _page_specs((2 * kvd,), npg, upg),
            out_specs=per_seq((rows, A_HEAD_DIM)),
            scratch_shapes=[pltpu.VMEM((rows, kvd), BF16), pltpu.VMEM((rows, 1), F32),
                            pltpu.VMEM((rows, 1), F32), pltpu.VMEM((rows, kvd), F32)]),
        out_shape=jax.ShapeDtypeStruct((nb, rows, A_HEAD_DIM), BF16),
        compiler_params=_params(("parallel", "arbitrary")),
        name="dsa_sample_attn",
    )(pt, q_s, scores, thr, sel_new, k_new, v_new, *([kv_pool] * upg))


def _mla_sample_kernel(pt_ref, ql_ref, qr_ref, cn_ref, rn_ref, *rest, upg, T):
    lat, rope = rest[:upg], rest[upg:2 * upg]
    o_ref, m_ref, l_ref, acc_ref = rest[2 * upg:]
    p = pl.program_id(1)
    rows = D_HEADS * T

    @pl.when(p == 0)
    def _():
        m_ref[...] = jnp.full(m_ref.shape, NEG_BIG, F32)
        l_ref[...] = jnp.zeros(l_ref.shape, F32)
        acc_ref[...] = jnp.zeros(acc_ref.shape, F32)

    ql, qr = ql_ref[...].astype(BF16), qr_ref[...]

    def scores(c, kr):
        return (lax.dot_general(ql, c, _NT, preferred_element_type=F32)
                + lax.dot_general(qr, kr, _NT, preferred_element_type=F32)) * MLA_SCALE

    cs = [x[...].astype(BF16) for x in lat]
    s = jnp.concatenate([scores(cs[u], rope[u][...].astype(BF16)) for u in range(upg)], axis=1)

    def pv(pb):
        out = jnp.dot(pb[:, :PAGE], cs[0], preferred_element_type=F32)
        for u in range(1, upg):
            out = out + jnp.dot(pb[:, u * PAGE:(u + 1) * PAGE], cs[u], preferred_element_type=F32)
        return out

    _online_update(m_ref, l_ref, acc_ref, s, pv)

    @pl.when(p == pl.num_programs(1) - 1)
    def _():
        cn = cn_ref[...]
        sn = scores(cn, rn_ref[...])
        jn = lax.broadcasted_iota(I32, (T, LANES), 1)
        tn = lax.broadcasted_iota(I32, (T, LANES), 0)
        sn = jnp.where(((jn <= tn) & (jn < T))[None], sn.reshape(D_HEADS, T, LANES), -jnp.inf).reshape(rows, LANES)
        _online_update(m_ref, l_ref, acc_ref, sn, lambda pb: jnp.dot(pb, cn, preferred_element_type=F32))
        o_ref[...] = (acc_ref[...] / l_ref[...]).astype(o_ref.dtype)


def mla_sample_attn(pt, q_lat, q_rope, c_new, r_new, lat_pool, rope_pool, *, npg, upg, T):
    nb = q_lat.shape[0]
    rows = D_HEADS * T
    per_seq = lambda shape: pl.BlockSpec((None,) + shape, lambda b, p, pt: (b, 0, 0))
    return pl.pallas_call(
        functools.partial(_mla_sample_kernel, upg=upg, T=T),
        grid_spec=pltpu.PrefetchScalarGridSpec(
            num_scalar_prefetch=1, grid=(nb, npg // upg),
            in_specs=[per_seq((rows, D_KV_LORA)), per_seq((rows, D_ROPE)),
                      per_seq((LANES, D_KV_LORA)), per_seq((LANES, D_ROPE))]
                     + _page_specs((D_KV_LORA,), npg, upg) + _page_specs((D_ROPE,), npg, upg),
            out_specs=per_seq((rows, D_KV_LORA)),
            scratch_shapes=[pltpu.VMEM((rows, 1), F32), pltpu.VMEM((rows, 1), F32),
                            pltpu.VMEM((rows, D_KV_LORA), F32)]),
        out_shape=jax.ShapeDtypeStruct((nb, rows, D_KV_LORA), F32),
        compiler_params=_params(("parallel", "arbitrary")),
        name="mla_sample_attn",
    )(pt, q_lat, q_rope, c_new, r_new, *([lat_pool] * upg), *([rope_pool] * upg))


def _head_mm_kernel(a_ref, w_ref, o_ref):
    nb, t, ka = a_ref.shape
    out = jnp.dot(a_ref[...].reshape(nb * t, ka).astype(BF16), w_ref[...], preferred_element_type=F32)
    o_ref[...] = out.reshape(nb, t, out.shape[1]).astype(o_ref.dtype)


def head_mm(a, w, T, out_dtype):
    nb, _, ka = a.shape
    nh, _, nbo = w.shape
    return pl.pallas_call(
        _head_mm_kernel,
        grid=(nh,),
        in_specs=[pl.BlockSpec((nb, T, ka), lambda h: (0, h, 0)),
                  pl.BlockSpec((None, ka, nbo), lambda h: (h, 0, 0))],
        out_specs=pl.BlockSpec((nb, T, nbo), lambda h: (0, h, 0)),
        out_shape=jax.ShapeDtypeStruct((nb, nh * T, nbo), out_dtype),
        compiler_params=_params(("parallel",)),
        name="head_mm",
    )(a, w)


def _flags(*runs):
    out = []
    for cnt, flag in runs:
        out += [flag] * cnt
    return jnp.array(out, I32)


def _pad_tokens(a, rows=LANES):
    nb, t, c = a.shape
    return jnp.concatenate([a, jnp.zeros((nb, rows - t, c), a.dtype)], axis=1)


def kernel(x_prompt, x_sample, cache_a_kv, cache_a_idx_k, state_b_kv, state_c_kv_0, state_c_kv_1, state_c_kv_2,
           cache_d_latent, cache_d_k_rope, page_table, norm_mix, norm_ffn, norm_final, a_w_in, a_w_out, b_w_in,
           b_sinks, b_w_out, c_w_in, c_w_out, d_w_in, d_q_norm, d_w_q_up, d_kv_norm, d_w_kv_up, d_w_out,
           ffn_w_in, ffn_w_out):
    B, S, DM = x_prompt.shape
    NB, T, _ = x_sample.shape
    NPG = page_table.shape[1]
    PAST = NPG * PAGE
    MP, MS = B * S, NB * T
    assert T == SUBLANES and S % QBLK == 0 and PAST % 512 == 0
    UPG = _pick(NPG, 8)

    x = jnp.concatenate([x_prompt.reshape(MP, DM), x_sample.reshape(MS, DM)], axis=0)
    pos = jnp.concatenate([jnp.tile(jnp.arange(S), B), jnp.tile(PAST + jnp.arange(T), NB)])
    cos128, sin128 = _rope_tables(pos, 128)
    cos64, sin64 = _rope_tables(pos, 64)
    pt = page_table.reshape(-1).astype(I32)

    def tok_rows(a, heads, dim):
        return a.reshape(NB, T, heads, dim).transpose(0, 2, 1, 3).reshape(NB, heads * T, dim)

    def rows_tok(a, heads, dim):
        return a.reshape(NB, heads, T, dim).transpose(0, 2, 1, 3).reshape(MS, heads * dim)

    def layer_tail(att, w_out, layer, xin):
        y = mm_res(att, w_out.astype(BF16), xin)
        return ffn(y, norm_ffn[layer][None], ffn_w_in[layer].astype(BF16), ffn_w_out[layer].astype(BF16))

    wq, wkv, wqi, wki, wwi = jnp.split(a_w_in, [2048, 3072, 5120, 5248], axis=1)
    wa = jnp.concatenate([wq, wqi, wkv, wki, wwi, jnp.zeros((DM, A_NP - a_w_in.shape[1]), F32)], axis=1)
    p32, p16 = norm_proj(x, 0, norm_mix[0][None], wa.astype(BF16),
                         _flags((16, 1), (16, 1), (4, 1), (4, 0), (1, 1), (1, 0)), cos128, sin128, 128)
    a_kv_p = p32[:MP, A_K0:A_KI0].reshape(B, S, 2, A_KV_HEADS, A_HEAD_DIM)
    a_kv_s = p32[MP:, A_K0:A_KI0].reshape(NB, T, 2, A_KV_HEADS, A_HEAD_DIM)
    a_ik_p = p32[:MP, A_KI0:A_WI0].reshape(B, S, A_IDX_DIM)
    a_ik_s = p32[MP:, A_KI0:A_WI0].reshape(NB, T, A_IDX_DIM)
    att_p = dsa_prompt(p16, p32, B, S)

    s16, s32 = p16[MP:], p32[MP:]
    qi_s = tok_rows(s16[:, A_QI0:A_K0], A_IDX_HEADS, A_IDX_DIM)
    w_s = s32[:, A_WI0:A_WI0 + A_IDX_HEADS].reshape(NB, T, A_IDX_HEADS).transpose(0, 2, 1)
    w_s = jnp.broadcast_to(w_s.reshape(NB, A_IDX_HEADS * T, 1), (NB, A_IDX_HEADS * T, LANES))
    scores = dsa_sample_scores(pt, qi_s, w_s, cache_a_idx_k, npg=NPG, upg=UPG, T=T)
    ki_new = _pad_tokens(s16[:, A_KI0:A_WI0].reshape(NB, T, A_IDX_DIM))
    thr, sel_new = dsa_sample_threshold(scores, qi_s, w_s, ki_new, T=T)
    kvd = A_KV_HEADS * A_HEAD_DIM
    att_s = dsa_sample_attn(
        pt, tok_rows(s16[:, A_Q0:A_QI0], A_HEADS, A_HEAD_DIM), scores, thr, sel_new,
        _pad_tokens(s16[:, A_K0:A_V0].reshape(NB, T, kvd)), _pad_tokens(s16[:, A_V0:A_KI0].reshape(NB, T, kvd)),
        cache_a_kv.reshape(cache_a_kv.shape[0], PAGE, 2 * kvd), npg=NPG, upg=UPG, T=T)
    att = jnp.concatenate([att_p, rows_tok(att_s, A_HEADS, A_HEAD_DIM)], axis=0)
    x = layer_tail(att, a_w_out, 0, x)

    hb, db, kb = B_HEADS, B_HEAD_DIM, B_KV_HEADS
    gb = hb // kb
    p32, p16 = norm_proj(x, 0, norm_mix[1][None], b_w_in.astype(BF16), _flags((16, 1), (2, 1), (2, 0)),
                         cos64, sin64, 64)
    q_w, kv_w = hb * db, 2 * kb * db
    att_p = band_attn(p16, p16, p16, nb=B, s=S, d=1, Hkv=kb, G=gb, Dh=db, Dv=db,
                      q_cb=lambda r, h: 0, k_cb=lambda r, h: q_w // (kb * db), v_cb=lambda r, h: q_w // (kb * db) + 1,
                      heads_in_grid=False, ck=QBLK, window=B_WINDOW, scale=db ** -0.5, sinks=b_sinks)
    b_kv_p = p32[:MP, q_w:q_w + kv_w].reshape(B, S, 2, kb, db)[:, S - min(B_WINDOW, S):]
    s16, s32 = p16[MP:], p32[MP:]
    kvn = s32[:, q_w:q_w + kv_w].reshape(NB, T, 2, kb, db)
    wb = state_b_kv.shape[1]
    q_s = s16[:, :q_w].reshape(NB, T, kb, gb, db).transpose(0, 2, 3, 1, 4).reshape(NB, kb, gb * T, db)
    buf_t = state_b_kv.transpose(0, 2, 3, 1, 4)
    sb = _pick(NB, 8)
    o_s, _ = win_sample_attn(
        q_s, buf_t, buf_t, lambda b, h, sk: (b, 0, h, 0, 0), lambda b, h, sk: (b, 1, h, 0, 0),
        (sb, None, None, wb, db), kvn[:, :, 0].transpose(0, 2, 1, 3), kvn[:, :, 1].transpose(0, 2, 1, 3),
        Hkv=kb, G=gb, T=T, Dh=db, wb=wb, window=B_WINDOW, dil=1, scale=db ** -0.5, sinks=b_sinks)
    o_s = o_s.reshape(NB, kb, gb, T, db).transpose(0, 3, 1, 2, 4).reshape(MS, q_w)
    b_kv_s = jnp.concatenate([state_b_kv, kvn], axis=1)[:, wb + T - min(B_WINDOW, PAST + T):]
    att = jnp.concatenate([att_p, o_s.astype(BF16)], axis=0)
    x = layer_tail(att, b_w_out, 1, x)

    hc, dc, kc = C_HEADS, C_HEAD_DIM, C_KV_HEADS
    gc = hc // kc
    gw = (hc + 2 * kc) * dc
    p32, p16 = norm_proj(x, 0, norm_mix[2][None], c_w_in.astype(BF16),
                         _flags(*([(hc, 1), (kc, 1), (kc, 0)] * len(C_GROUPS))), cos128, sin128, 128)
    s16, s32 = p16[MP:], p32[MP:]
    ncol = p16.shape[1]
    outs, lses, c_states = [], [], []
    for gi, ((window, dil), buf) in enumerate(zip(C_GROUPS, (state_c_kv_0, state_c_kv_1, state_c_kv_2))):
        q0, k0 = gi * gw, gi * gw + hc * dc
        v0 = k0 + kc * dc
        o_p, lse_p = band_attn(
            p16, p16, p16, nb=B, s=S, d=dil, Hkv=kc, G=gc, Dh=dc, Dv=dc,
            q_cb=lambda r, h, q0=q0: (r * ncol + q0) // (gc * dc) + h,
            k_cb=lambda r, h, k0=k0: (r * ncol + k0) // dc + h,
            v_cb=lambda r, h, v0=v0: (r * ncol + v0) // dc + h,
            heads_in_grid=True, ck=QBLK, window=window // dil, scale=dc ** -0.5, want_lse=True, out_dtype=F32)
        c_p = p32[:MP, k0:k0 + 2 * kc * dc].reshape(B, S, 2, kc, dc)[:, S - min(window, S):]
        kvn = s32[:, k0:k0 + 2 * kc * dc].reshape(NB, T, 2, kc, dc)
        wb = buf.shape[1]
        q_s = s16[:, q0:k0].reshape(NB, T, kc, gc, dc).transpose(0, 2, 3, 1, 4).reshape(NB, kc, gc * T, dc)
        buf_v = buf.reshape(NB, wb, 2 * kc * dc)
        sb = _pick(NB, 8 if wb <= 512 else 2)
        o_s, lse_s = win_sample_attn(
            q_s, buf_v, buf_v, lambda b, h, sk: (b, 0, h), lambda b, h, sk: (b, 0, kc + h), (sb, wb, dc),
            kvn[:, :, 0].transpose(0, 2, 1, 3), kvn[:, :, 1].transpose(0, 2, 1, 3),
            Hkv=kc, G=gc, T=T, Dh=dc, wb=wb, window=window, dil=dil, scale=dc ** -0.5)
        untok = lambda a: a.reshape(NB, kc, gc, T, dc).transpose(0, 3, 1, 2, 4).reshape(MS, hc * dc)
        outs.append(jnp.concatenate([o_p, untok(o_s)], axis=0))
        lses.append(jnp.concatenate([lse_p, untok(lse_s)], axis=0))
        c_states += [c_p, jnp.concatenate([buf, kvn], axis=1)[:, wb + T - min(window, PAST + T):]]
    x = merge_mm_res(outs, lses, c_w_out.astype(BF16), x)
    x = ffn(x, norm_ffn[2][None], ffn_w_in[2].astype(BF16), ffn_w_out[2].astype(BF16))

    hd = D_HEADS
    dpad = 2 * LANES
    wd = jnp.concatenate([d_w_in, jnp.zeros((DM, LANES - D_ROPE), F32)], axis=1)
    d32, d16 = norm_proj(x, 0, norm_mix[3][None], wd.astype(BF16), _flags((8, 0), (1, 1)), cos64, sin64, 64)
    wqu = d_w_q_up.reshape(D_Q_LORA, hd, D_NOPE + D_ROPE)
    wqu = jnp.concatenate([wqu, jnp.zeros((D_Q_LORA, hd, dpad - D_NOPE - D_ROPE), F32)], axis=2)
    q32, q16 = norm_proj(d32, 0, d_q_norm[None], wqu.reshape(D_Q_LORA, hd * dpad).astype(BF16),
                         _flags(*([(1, 0), (1, 1)] * hd)), cos64, sin64, 64)
    c_kv = rmsnorm(d32, 1, d_kv_norm[None])
    d_lat_p = c_kv[:MP].reshape(B, S, D_KV_LORA)
    d_lat_s = c_kv[MP:].reshape(NB, T, D_KV_LORA)
    kr0 = D_Q_LORA + D_KV_LORA
    d_kr_p = d32[:MP, kr0:kr0 + D_ROPE].reshape(B, S, D_ROPE)
    d_kr_s = d32[MP:, kr0:kr0 + D_ROPE].reshape(NB, T, D_ROPE)
    _, kv16 = norm_proj(d32, 1, d_kv_norm[None], d_w_kv_up.astype(BF16), jnp.zeros((hd * 2,), I32),
                        cos64, sin64, 64, rows=MP)
    att_p = band_attn(q16, kv16, kv16, nb=B, s=S, d=1, Hkv=hd, G=1, Dh=D_NOPE, Dv=D_V,
                      q_cb=lambda r, h: h, k_cb=lambda r, h: 2 * h, v_cb=lambda r, h: 2 * h + 1,
                      heads_in_grid=True, ck=min(512, S), window=None, scale=MLA_SCALE,
                      k2arr=d16, k2_cb=lambda r, h: kr0 // LANES, Dr=LANES)
    qs = q32[MP:].reshape(NB, T, hd, dpad)
    qn_s = qs[..., :D_NOPE].transpose(0, 2, 1, 3).reshape(NB, hd * T, D_NOPE)
    qr_s = qs[..., D_NOPE:D_NOPE + D_ROPE].transpose(0, 2, 1, 3).reshape(NB, hd * T, D_ROPE).astype(BF16)
    w_kv = d_w_kv_up.reshape(D_KV_LORA, hd, D_NOPE + D_V)
    w_uk_t = w_kv[..., :D_NOPE].transpose(1, 2, 0).astype(BF16)
    w_uv = w_kv[..., D_NOPE:].transpose(1, 0, 2).astype(BF16)
    q_lat = head_mm(qn_s, w_uk_t, T, F32)
    o_lat = mla_sample_attn(
        pt, q_lat, qr_s, _pad_tokens(c_kv[MP:].reshape(NB, T, D_KV_LORA)).astype(BF16),
        _pad_tokens(d16[MP:, kr0:kr0 + D_ROPE].reshape(NB, T, D_ROPE)),
        cache_d_latent, cache_d_k_rope, npg=NPG, upg=UPG, T=T)
    o_s = head_mm(o_lat, w_uv, T, F32)
    att = jnp.concatenate([att_p, rows_tok(o_s, hd, D_V).astype(BF16)], axis=0)
    x = layer_tail(att, d_w_out, 3, x)

    y = rmsnorm(x, 0, norm_final[None])
    return (y[:MP].reshape(B, S, DM), y[MP:].reshape(NB, T, DM), a_kv_p, a_kv_s, a_ik_p, a_ik_s, b_kv_p, b_kv_s,
            *c_states, d_lat_p, d_lat_s, d_kr_p, d_kr_s)
```

```python
import functools
import math

import jax
import jax.numpy as jnp
from jax import lax
from jax.experimental import pallas as pl
from jax.experimental.pallas import tpu as pltpu

F32 = jnp.float32
BF16 = jnp.bfloat16
I32 = jnp.int32

LANES = 128
SUBLANES = 8
QBLK = 128
PAGE = 128
NORM_EPS = 1e-6
ROPE_THETA = 10000.0
NEG_BIG = -1e30
VMEM_LIMIT = 56 * 1024 * 1024

A_HEADS, A_HEAD_DIM, A_KV_HEADS, A_IDX_HEADS, A_IDX_DIM, A_TOPK = 16, 128, 4, 16, 128, 256
A_IDX_SCALE = (A_IDX_HEADS * A_IDX_DIM) ** -0.5
B_HEADS, B_HEAD_DIM, B_KV_HEADS, B_WINDOW = 32, 64, 4, 128
C_GROUPS = ((128, 1), (512, 4), (2048, 16))
C_HEADS, C_KV_HEADS, C_HEAD_DIM = 8, 2, 128
D_HEADS, D_NOPE, D_ROPE, D_V, D_Q_LORA, D_KV_LORA = 16, 128, 64, 128, 512, 512
MLA_SCALE = (D_NOPE + D_ROPE) ** -0.5


def _params(sem):
    return pltpu.CompilerParams(dimension_semantics=sem, vmem_limit_bytes=VMEM_LIMIT)


def _pick(total, pref):
    t = min(pref, total)
    while total % t:
        t //= 2
    return t


def _rope_tables(pos, dh):
    inv = jnp.power(ROPE_THETA, -jnp.arange(0, dh, 2, dtype=F32) / dh)
    ang = pos.astype(F32)[:, None] * inv[None, :]
    cos, sin = jnp.cos(ang), jnp.sin(ang)
    c = jnp.concatenate([cos, cos], axis=-1)
    s = jnp.concatenate([-sin, sin], axis=-1)
    rep = LANES // dh
    return jnp.tile(c, (1, rep)), jnp.tile(s, (1, rep))


def _rope_block(x, cos, sin, dh):
    half = dh // 2
    if dh == LANES:
        rot = pltpu.roll(x, half, axis=1)
    else:
        lane = lax.broadcasted_iota(I32, x.shape, 1)
        first = (lane % dh) < half
        rot = jnp.where(first, pltpu.roll(x, LANES - half, axis=1), pltpu.roll(x, half, axis=1))
    return x * cos + rot * sin


def _norm_proj_kernel(flags_ref, x_ref, g_ref, w_ref, cos_ref, sin_ref, o32_ref, o16_ref, xn_ref,
                      *, dh, nsub):
    j = pl.program_id(1)

    @pl.when(j == 0)
    def _():
        xf = x_ref[...]
        y = xf * lax.rsqrt(jnp.mean(xf * xf, axis=-1, keepdims=True) + NORM_EPS)
        xn_ref[...] = (y * g_ref[...]).astype(BF16)

    acc = jnp.dot(xn_ref[...], w_ref[...], preferred_element_type=F32)
    for u in range(nsub):
        blk = acc[:, u * LANES:(u + 1) * LANES]
        roped = _rope_block(blk, cos_ref[...], sin_ref[...], dh)
        out = jnp.where(flags_ref[j * nsub + u] > 0, roped, blk)
        o32_ref[:, u * LANES:(u + 1) * LANES] = out
        o16_ref[:, u * LANES:(u + 1) * LANES] = out.astype(BF16)


def norm_proj(x, xcb, g, w, flags, cos, sin, dh, rows=None, tm=512, tn=512):
    m = x.shape[0] if rows is None else rows
    k, n = w.shape
    tm = _pick(m, tm)
    tn = _pick(n, tn)
    nsub = tn // LANES
    grid_spec = pltpu.PrefetchScalarGridSpec(
        num_scalar_prefetch=1,
        grid=(m // tm, n // tn),
        in_specs=[
            pl.BlockSpec((tm, k), lambda i, j, f: (i, xcb)),
            pl.BlockSpec((1, k), lambda i, j, f: (0, 0)),
            pl.BlockSpec((k, tn), lambda i, j, f: (0, j)),
            pl.BlockSpec((tm, LANES), lambda i, j, f: (i, 0)),
            pl.BlockSpec((tm, LANES), lambda i, j, f: (i, 0)),
        ],
        out_specs=[
            pl.BlockSpec((tm, tn), lambda i, j, f: (i, j)),
            pl.BlockSpec((tm, tn), lambda i, j, f: (i, j)),
        ],
        scratch_shapes=[pltpu.VMEM((tm, k), BF16)],
    )
    return pl.pallas_call(
        functools.partial(_norm_proj_kernel, dh=dh, nsub=nsub),
        grid_spec=grid_spec,
        out_shape=[jax.ShapeDtypeStruct((m, n), F32), jax.ShapeDtypeStruct((m, n), BF16)],
        compiler_params=_params(("parallel", "arbitrary")),
        name="norm_proj",
    )(flags, x, g, w, cos, sin)


def _rmsnorm_kernel(x_ref, g_ref, o_ref):
    xf = x_ref[...]
    y = xf * lax.rsqrt(jnp.mean(xf * xf, axis=-1, keepdims=True) + NORM_EPS)
    o_ref[...] = y * g_ref[...]


def rmsnorm(x, xcb, g, tm=512):
    m = x.shape[0]
    k = g.shape[1]
    tm = _pick(m, tm)
    return pl.pallas_call(
        _rmsnorm_kernel,
        grid=(m // tm,),
        in_specs=[pl.BlockSpec((tm, k), lambda i: (i, xcb)), pl.BlockSpec((1, k), lambda i: (0, 0))],
        out_specs=pl.BlockSpec((tm, k), lambda i: (i, 0)),
        out_shape=jax.ShapeDtypeStruct((m, k), F32),
        compiler_params=_params(("parallel",)),
        name="rmsnorm",
    )(x, g)


def _mm_res_kernel(a_ref, w_ref, r_ref, o_ref):
    o_ref[...] = r_ref[...] + jnp.dot(a_ref[...].astype(BF16), w_ref[...], preferred_element_type=F32)


def mm_res(a, w, res, tm=512, tn=512):
    m, k = a.shape
    n = w.shape[1]
    tm = _pick(m, tm)
    tn = _pick(n, tn)
    return pl.pallas_call(
        _mm_res_kernel,
        grid=(m // tm, n // tn),
        in_specs=[
            pl.BlockSpec((tm, k), lambda i, j: (i, 0)),
            pl.BlockSpec((k, tn), lambda i, j: (0, j)),
            pl.BlockSpec((tm, tn), lambda i, j: (i, j)),
        ],
        out_specs=pl.BlockSpec((tm, tn), lambda i, j: (i, j)),
        out_shape=jax.ShapeDtypeStruct((m, n), F32),
        compiler_params=_params(("parallel", "parallel")),
        name="mm_res",
    )(a, w, res)


def _merge_mm_res_kernel(o0_ref, o1_ref, o2_ref, l0_ref, l1_ref, l2_ref, w_ref, r_ref, out_ref, mg_ref):
    j = pl.program_id(1)

    @pl.when(j == 0)
    def _():
        l0, l1, l2 = l0_ref[...], l1_ref[...], l2_ref[...]
        mx = jnp.maximum(jnp.maximum(l0, l1), l2)
        e0, e1, e2 = jnp.exp(l0 - mx), jnp.exp(l1 - mx), jnp.exp(l2 - mx)
        den = e0 + e1 + e2
        mg = (e0 / den) * o0_ref[...] + (e1 / den) * o1_ref[...] + (e2 / den) * o2_ref[...]
        mg_ref[...] = mg.astype(BF16)

    out_ref[...] = r_ref[...] + jnp.dot(mg_ref[...], w_ref[...], preferred_element_type=F32)


def merge_mm_res(outs, lses, w, res, tm=256, tn=512):
    m, k = outs[0].shape
    n = w.shape[1]
    tm = _pick(m, tm)
    tn = _pick(n, tn)
    row = pl.BlockSpec((tm, k), lambda i, j: (i, 0))
    return pl.pallas_call(
        _merge_mm_res_kernel,
        grid=(m // tm, n // tn),
        in_specs=[row] * 6 + [
            pl.BlockSpec((k, tn), lambda i, j: (0, j)),
            pl.BlockSpec((tm, tn), lambda i, j: (i, j)),
        ],
        out_specs=pl.BlockSpec((tm, tn), lambda i, j: (i, j)),
        out_shape=jax.ShapeDtypeStruct((m, n), F32),
        scratch_shapes=[pltpu.VMEM((tm, k), BF16)],
        compiler_params=_params(("parallel", "arbitrary")),
        name="merge_mm_res",
    )(*outs, *lses, w, res)


def _ffn_kernel(x_ref, g_ref, wg_ref, wu_ref, wo_ref, o_ref, xn_ref, acc_ref):
    f = pl.program_id(1)

    @pl.when(f == 0)
    def _():
        xf = x_ref[...]
        y = xf * lax.rsqrt(jnp.mean(xf * xf, axis=-1, keepdims=True) + NORM_EPS)
        xn_ref[...] = (y * g_ref[...]).astype(BF16)
        acc_ref[...] = xf

    xn = xn_ref[...]
    gate = jnp.dot(xn, wg_ref[...], preferred_element_type=F32)
    up = jnp.dot(xn, wu_ref[...], preferred_element_type=F32)
    h = (gate * jax.nn.sigmoid(gate) * up).astype(BF16)
    acc_ref[...] += jnp.dot(h, wo_ref[...], preferred_element_type=F32)

    @pl.when(f == pl.num_programs(1) - 1)
    def _():
        o_ref[...] = acc_ref[...]


def ffn(x, g, w_in, w_out, tm=512, tf=512):
    m, d = x.shape
    ff = w_out.shape[0]
    tm = _pick(m, tm)
    tf = _pick(ff, tf)
    nf = ff // tf
    return pl.pallas_call(
        _ffn_kernel,
        grid=(m // tm, nf),
        in_specs=[
            pl.BlockSpec((tm, d), lambda i, f: (i, 0)),
            pl.BlockSpec((1, d), lambda i, f: (0, 0)),
            pl.BlockSpec((d, tf), lambda i, f: (0, f)),
            pl.BlockSpec((d, tf), lambda i, f: (0, f + nf)),
            pl.BlockSpec((tf, d), lambda i, f: (f, 0)),
        ],
        out_specs=pl.BlockSpec((tm, d), lambda i, f: (i, 0)),
        out_shape=jax.ShapeDtypeStruct((m, d), F32),
        scratch_shapes=[pltpu.VMEM((tm, d), BF16), pltpu.VMEM((tm, d), F32)],
        compiler_params=_params(("parallel", "arbitrary")),
        name="ffn",
    )(x, g, w_in, w_in, w_out)


_NT = (((1,), (1,)), ((), ()))


def _band_attn_kernel(sink_ref, q_ref, k_ref, v_ref, o_ref, *lse_refs, G, Dh, Dv, nheads, ck, window, scale,
                      has_sink, heads_in_grid):
    qb = pl.program_id(3 if heads_in_grid else 2)
    head0 = pl.program_id(2) if heads_in_grid else 0
    rows = G * QBLK
    q0 = qb * QBLK
    c_hi = (q0 + QBLK - 1) // ck
    c_lo = jnp.maximum(q0 - window, 0) // ck
    row = lax.broadcasted_iota(I32, (QBLK, ck), 0)
    col = lax.broadcasted_iota(I32, (QBLK, ck), 1)
    qs = [jnp.concatenate([q_ref[:, (hh * G + g) * Dh:(hh * G + g + 1) * Dh] for g in range(G)], axis=0)
          for hh in range(nheads)]

    def step(c, carry):
        off = pl.multiple_of(c * ck, ck)
        dist = (q0 + row) - (off + col)
        valid = (dist >= 0) & (dist <= window)
        out = []
        for hh in range(nheads):
            m, l, acc = carry[hh]
            s = lax.dot_general(qs[hh], k_ref[pl.ds(off, ck), hh * Dh:(hh + 1) * Dh], _NT,
                                preferred_element_type=F32) * scale
            s = jnp.where(valid[None], s.reshape(G, QBLK, ck), -jnp.inf).reshape(rows, ck)
            m_new = jnp.maximum(m, jnp.max(s, axis=1, keepdims=True))
            alpha = jnp.exp(m - m_new)
            p = jnp.exp(s - m_new)
            l = alpha * l + jnp.sum(p, axis=1, keepdims=True)
            acc = alpha * acc + jnp.dot(p.astype(BF16), v_ref[pl.ds(off, ck), hh * Dv:(hh + 1) * Dv],
                                        preferred_element_type=F32)
            out.append((m_new, l, acc))
        return tuple(out)

    init = []
    for hh in range(nheads):
        if has_sink:
            m0 = jnp.concatenate(
                [jnp.full((QBLK, 1), sink_ref[(head0 + hh) * G + g], F32) for g in range(G)], axis=0)
            l0 = jnp.ones((rows, 1), F32)
        else:
            m0 = jnp.full((rows, 1), NEG_BIG, F32)
            l0 = jnp.zeros((rows, 1), F32)
        init.append((m0, l0, jnp.zeros((rows, Dv), F32)))
    res = lax.fori_loop(c_lo, c_hi + 1, step, tuple(init))
    for hh in range(nheads):
        m, l, acc = res[hh]
        o = acc / l
        lse = m + jnp.log(l)
        for g in range(G):
            lo = (hh * G + g) * Dv
            o_ref[:, lo:lo + Dv] = o[g * QBLK:(g + 1) * QBLK].astype(o_ref.dtype)
            if lse_refs:
                lse_refs[0][:, lo:lo + Dv] = jnp.broadcast_to(lse[g * QBLK:(g + 1) * QBLK], (QBLK, Dv))


def band_attn(qarr, karr, varr, *, nb, s, d, Hkv, G, Dh, Dv, q_cb, k_cb, v_cb, heads_in_grid, ck, window,
              scale, sinks=None, want_lse=False, out_dtype=BF16):
    s_sub = s // d
    nqb = s_sub // QBLK
    nheads = 1 if heads_in_grid else Hkv
    hq = Hkv * G

    def view(a):
        return a.reshape(a.shape[0] // d, d * a.shape[1])

    if heads_in_grid:
        grid = (nb, d, Hkv, nqb)
        qmap = lambda b, r, h, i, sk: (b * nqb + i, q_cb(r, h))
        kmap = lambda b, r, h, i, sk: (b, k_cb(r, h))
        vmap = lambda b, r, h, i, sk: (b, v_cb(r, h))
        omap = lambda b, r, h, i, sk: (b * nqb + i, r * Hkv + h)
        sem = ("parallel", "parallel", "parallel", "parallel")
    else:
        grid = (nb, d, nqb)
        qmap = lambda b, r, i, sk: (b * nqb + i, q_cb(r, 0))
        kmap = lambda b, r, i, sk: (b, k_cb(r, 0))
        vmap = lambda b, r, i, sk: (b, v_cb(r, 0))
        omap = lambda b, r, i, sk: (b * nqb + i, r)
        sem = ("parallel", "parallel", "parallel")

    in_specs = [
        pl.BlockSpec((QBLK, nheads * G * Dh), qmap),
        pl.BlockSpec((s_sub, nheads * Dh), kmap),
        pl.BlockSpec((s_sub, nheads * Dv), vmap),
    ]
    out_specs = [pl.BlockSpec((QBLK, nheads * G * Dv), omap)]
    out_shape = [jax.ShapeDtypeStruct((nb * s_sub, d * hq * Dv), out_dtype)]
    if want_lse:
        out_specs.append(pl.BlockSpec((QBLK, nheads * G * Dv), omap))
        out_shape.append(jax.ShapeDtypeStruct((nb * s_sub, d * hq * Dv), F32))
    has_sink = sinks is not None
    if sinks is None:
        sinks = jnp.zeros((hq,), F32)
    res = pl.pallas_call(
        functools.partial(_band_attn_kernel, G=G, Dh=Dh, Dv=Dv, nheads=nheads, ck=ck, window=window,
                          scale=scale, has_sink=has_sink, heads_in_grid=heads_in_grid),
        grid_spec=pltpu.PrefetchScalarGridSpec(
            num_scalar_prefetch=1, grid=grid, in_specs=in_specs, out_specs=out_specs),
        out_shape=out_shape,
        compiler_params=_params(sem),
        name="band_attn",
    )(sinks.astype(F32), view(qarr), view(karr), view(varr))
    res = [a.reshape(nb * s, hq * Dv) for a in res]
    return res if want_lse else res[0]


MLA_QW = 2 * LANES
MLA_QROWS = 256
MLA_CK = 512
MLA_HS = 2


def _mla_prompt_kernel(q_ref, kv_ref, k2_ref, o_ref, kcat_ref, *, hs, qrows, ck):
    i = pl.program_id(2)

    @pl.when(i == 0)
    def _():
        for h in range(hs):
            kcat_ref[h, :, :D_NOPE] = kv_ref[:, h * MLA_QW:h * MLA_QW + D_NOPE]
            kcat_ref[h, :, D_NOPE:] = k2_ref[...]

    q0 = i * qrows
    n_full = q0 // ck
    row = lax.broadcasted_iota(I32, (qrows, ck), 0)
    col = lax.broadcasted_iota(I32, (qrows, ck), 1)

    def chunk(c, carry, masked):
        off = pl.multiple_of(c * ck, ck)
        out = []
        for h in range(hs):
            m, l, acc = carry[h]
            s = lax.dot_general(q_ref[:, h * MLA_QW:(h + 1) * MLA_QW], kcat_ref[h, pl.ds(off, ck), :], _NT,
                                preferred_element_type=F32) * MLA_SCALE
            if masked:
                s = jnp.where((off + col) <= (q0 + row), s, -jnp.inf)
            m_new = jnp.maximum(m, jnp.max(s, axis=1, keepdims=True))
            alpha = jnp.exp(m - m_new)
            p = jnp.exp(s - m_new)
            l = alpha * l + jnp.sum(p, axis=1, keepdims=True)
            v = kv_ref[pl.ds(off, ck), h * MLA_QW + D_NOPE:(h + 1) * MLA_QW]
            acc = alpha * acc + jnp.dot(p.astype(BF16), v, preferred_element_type=F32)
            out.append((m_new, l, acc))
        return tuple(out)

    init = tuple((jnp.full((qrows, 1), NEG_BIG, F32), jnp.zeros((qrows, 1), F32), jnp.zeros((qrows, D_V), F32))
                 for _ in range(hs))
    carry = lax.fori_loop(0, n_full, lambda c, cr: chunk(c, cr, False), init)
    carry = chunk(n_full, carry, True)
    for h in range(hs):
        _, l, acc = carry[h]
        o_ref[:, h * D_V:(h + 1) * D_V] = (acc / l).astype(o_ref.dtype)


def mla_prompt_attn(q16, kv16, d16, kr_cb, nb, s):
    qrows, ck, hs = min(MLA_QROWS, s), min(MLA_CK, s), MLA_HS
    assert ck % qrows == 0 and s % ck == 0 and D_HEADS % hs == 0 and D_NOPE == D_V == LANES
    nq = s // qrows
    return pl.pallas_call(
        functools.partial(_mla_prompt_kernel, hs=hs, qrows=qrows, ck=ck),
        grid=(nb, D_HEADS // hs, nq),
        in_specs=[pl.BlockSpec((qrows, hs * MLA_QW), lambda b, h, i: (b * nq + i, h)),
                  pl.BlockSpec((s, hs * MLA_QW), lambda b, h, i: (b, h)),
                  pl.BlockSpec((s, LANES), lambda b, h, i: (b, kr_cb))],
        out_specs=pl.BlockSpec((qrows, hs * D_V), lambda b, h, i: (b * nq + i, h)),
        out_shape=jax.ShapeDtypeStruct((nb * s, D_HEADS * D_V), BF16),
        scratch_shapes=[pltpu.VMEM((hs, s, MLA_QW), BF16)],
        compiler_params=_params(("parallel", "parallel", "arbitrary")),
        name="mla_prompt_attn",
    )(q16, kv16, d16)


INT_MIN = -2 ** 31


def _float_key(x):
    i = pltpu.bitcast(x + 0.0, I32)
    return i ^ ((i >> 31) & 0x7FFFFFFF)


def _kth_key(count_ge, rows, k):
    def body(it, t):
        bit = 31 - it
        cand = t + lax.shift_left(jnp.int32(1), bit)
        return jnp.where(count_ge(cand) >= k, cand, t)

    return lax.fori_loop(0, 32, body, jnp.full((rows, 1), INT_MIN, I32))


def _dsa_prompt_kernel(q_ref, qi_ref, wi_ref, ki_ref, k_ref, v_ref, o_ref, key_ref, *, ck, topk, scale):
    qb = pl.program_id(1)
    q0 = qb * QBLK
    nk = (q0 + QBLK - 1) // ck + 1
    row = lax.broadcasted_iota(I32, (QBLK, ck), 0)
    col = lax.broadcasted_iota(I32, (QBLK, ck), 1)
    w = wi_ref[...]
    g = A_HEADS // A_KV_HEADS
    dh = A_HEAD_DIM

    def score_chunk(c, carry):
        off = pl.multiple_of(c * ck, ck)
        kic = ki_ref[pl.ds(off, ck), :]
        acc = jnp.zeros((QBLK, ck), F32)
        for h in range(A_IDX_HEADS):
            r = lax.dot_general(qi_ref[:, h * A_IDX_DIM:(h + 1) * A_IDX_DIM], kic, _NT,
                                preferred_element_type=F32)
            acc = acc + jnp.maximum(r, 0.0) * w[:, h:h + 1]
        sc = jnp.where((off + col) <= (q0 + row), acc * A_IDX_SCALE, -jnp.inf)
        key_ref[:, pl.ds(off, ck)] = _float_key(sc)
        return carry

    lax.fori_loop(0, nk, score_chunk, 0)

    def count_ge(cand):
        def inner(c, cnt):
            off = pl.multiple_of(c * ck, ck)
            ge = (key_ref[:, pl.ds(off, ck)] >= cand).astype(I32)
            for u in range(ck // LANES):
                cnt = cnt + ge[:, u * LANES:(u + 1) * LANES]
            return cnt
        cnt = lax.fori_loop(0, nk, inner, jnp.zeros((QBLK, LANES), I32))
        return jnp.sum(cnt, axis=1, keepdims=True)

    thr = _kth_key(count_ge, QBLK, topk)
    rows = g * QBLK
    qs = [jnp.concatenate([q_ref[:, (hh * g + j) * dh:(hh * g + j + 1) * dh] for j in range(g)], axis=0)
          for hh in range(A_KV_HEADS)]

    def step(c, carry):
        off = pl.multiple_of(c * ck, ck)
        sel = (key_ref[:, pl.ds(off, ck)] >= thr) & ((off + col) <= (q0 + row))
        out = []
        for hh in range(A_KV_HEADS):
            m, l, acc = carry[hh]
            s = lax.dot_general(qs[hh], k_ref[pl.ds(off, ck), hh * dh:(hh + 1) * dh], _NT,
                                preferred_element_type=F32) * scale
            s = jnp.where(sel[None], s.reshape(g, QBLK, ck), -jnp.inf).reshape(rows, ck)
            m_new = jnp.maximum(m, jnp.max(s, axis=1, keepdims=True))
            alpha = jnp.exp(m - m_new)
            p = jnp.exp(s - m_new)
            l = alpha * l + jnp.sum(p, axis=1, keepdims=True)
            acc = alpha * acc + jnp.dot(p.astype(BF16), v_ref[pl.ds(off, ck), hh * dh:(hh + 1) * dh],
                                        preferred_element_type=F32)
            out.append((m_new, l, acc))
        return tuple(out)

    init = tuple((jnp.full((rows, 1), NEG_BIG, F32), jnp.zeros((rows, 1), F32), jnp.zeros((rows, dh), F32))
                 for _ in range(A_KV_HEADS))
    res = lax.fori_loop(0, nk, step, init)
    for hh in range(A_KV_HEADS):
        _, l, acc = res[hh]
        o = acc / l
        for j in range(g):
            lo = (hh * g + j) * dh
            o_ref[:, lo:lo + dh] = o[j * QBLK:(j + 1) * QBLK].astype(o_ref.dtype)


A_Q0, A_QI0, A_K0, A_V0, A_KI0, A_WI0, A_NP = 0, 2048, 4096, 4608, 5120, 5248, 5376


def dsa_prompt(p16, p32, nb, s):
    nqb = s // QBLK
    ck = min(512, s)
    topk = min(A_TOPK, s // 4)
    hd = A_HEADS * A_HEAD_DIM
    kvd = A_KV_HEADS * A_HEAD_DIM
    return pl.pallas_call(
        functools.partial(_dsa_prompt_kernel, ck=ck, topk=topk, scale=A_HEAD_DIM ** -0.5),
        grid=(nb, nqb),
        in_specs=[
            pl.BlockSpec((QBLK, hd), lambda b, i: (b * nqb + i, A_Q0 // hd)),
            pl.BlockSpec((QBLK, hd), lambda b, i: (b * nqb + i, A_QI0 // hd)),
            pl.BlockSpec((QBLK, LANES), lambda b, i: (b * nqb + i, A_WI0 // LANES)),
            pl.BlockSpec((s, A_IDX_DIM), lambda b, i: (b, A_KI0 // A_IDX_DIM)),
            pl.BlockSpec((s, kvd), lambda b, i: (b, A_K0 // kvd)),
            pl.BlockSpec((s, kvd), lambda b, i: (b, A_V0 // kvd)),
        ],
        out_specs=pl.BlockSpec((QBLK, hd), lambda b, i: (b * nqb + i, 0)),
        out_shape=jax.ShapeDtypeStruct((nb * s, hd), BF16),
        scratch_shapes=[pltpu.VMEM((QBLK, s), I32)],
        compiler_params=_params(("parallel", "arbitrary")),
        name="dsa_prompt",
    )(p16, p16, p32, p16, p16, p16)


def _pad_rows(a, rows):
    return jnp.concatenate([a, jnp.zeros((rows - a.shape[0], a.shape[1]), a.dtype)], axis=0)


def _win_sample_kernel(sink_ref, q_ref, buf_ref, new_ref, o_ref, lse_ref, st_ref, *, sb, Hkv, G, T, wb, off,
                       window, dil, scale, has_sink):
    rows = G * T
    ncol = wb + LANES
    col = lax.broadcasted_iota(I32, (T, ncol), 1)
    t = lax.broadcasted_iota(I32, (T, ncol), 0)
    dist = wb + t - col
    valid = (dist >= 0) & (dist <= window) & ((dist & (dil - 1)) == 0)
    rpp = 2 * Hkv
    for i in range(sb):
        for h in range(Hkv):
            q = q_ref[i, h]
            kb = buf_ref[i, pl.ds(h, wb, stride=rpp), :].astype(BF16)
            vb = buf_ref[i, pl.ds(Hkv + h, wb, stride=rpp), :].astype(BF16)
            kn = _pad_rows(new_ref[i, pl.ds(h, T, stride=rpp), :], LANES).astype(BF16)
            vn = _pad_rows(new_ref[i, pl.ds(Hkv + h, T, stride=rpp), :], LANES).astype(BF16)
            s = jnp.concatenate([lax.dot_general(q, kb, _NT, preferred_element_type=F32),
                                 lax.dot_general(q, kn, _NT, preferred_element_type=F32)], axis=1) * scale
            s = jnp.where(valid[None], s.reshape(G, T, ncol), -jnp.inf).reshape(rows, ncol)
            m = jnp.max(s, axis=1, keepdims=True)
            if has_sink:
                sink = jnp.concatenate([jnp.full((T, 1), sink_ref[h * G + g], F32) for g in range(G)], axis=0)
                m = jnp.maximum(m, sink)
            e = jnp.exp(s - m)
            den = jnp.sum(e, axis=1, keepdims=True)
            if has_sink:
                den = den + jnp.exp(sink - m)
            eb = e.astype(BF16)
            acc = (jnp.dot(eb[:, :wb], vb, preferred_element_type=F32)
                   + jnp.dot(eb[:, wb:], vn, preferred_element_type=F32))
            o_ref[i, h] = acc / den
            lse_ref[i, h] = jnp.broadcast_to(m + jnp.log(den), (rows, o_ref.shape[-1]))
        st_ref[i, pl.ds(0, (wb - off) * rpp), :] = buf_ref[i, pl.ds(off * rpp, (wb - off) * rpp), :]
        st_ref[i, pl.ds((wb - off) * rpp, T * rpp), :] = new_ref[i]


def win_sample_attn(q, buf, new, *, sb, window, dil, past, scale, sinks=None):
    nb, hkv, rows, dh = q.shape
    wb, t = buf.shape[1], new.shape[1]
    g = rows // t
    rpp = 2 * hkv
    keep = min(window, past + t)
    off = wb + t - keep
    sb = _pick(nb, sb)
    has_sink = sinks is not None
    if sinks is None:
        sinks = jnp.zeros((hkv * g,), F32)
    qspec = pl.BlockSpec((sb, hkv, rows, dh), lambda b, sk: (b, 0, 0, 0))
    oshape = jax.ShapeDtypeStruct((nb, hkv, rows, dh), F32)
    flat = lambda n: pl.BlockSpec((sb, n * rpp, dh), lambda b, sk: (b, 0, 0))
    o, lse, st = pl.pallas_call(
        functools.partial(_win_sample_kernel, sb=sb, Hkv=hkv, G=g, T=t, wb=wb, off=off, window=window, dil=dil,
                          scale=scale, has_sink=has_sink),
        grid_spec=pltpu.PrefetchScalarGridSpec(
            num_scalar_prefetch=1, grid=(nb // sb,),
            in_specs=[qspec, flat(wb), flat(t)],
            out_specs=[qspec, qspec, flat(keep)]),
        out_shape=[oshape, oshape, jax.ShapeDtypeStruct((nb, keep * rpp, dh), F32)],
        compiler_params=_params(("parallel",)),
        name="win_sample_attn",
    )(sinks.astype(F32), q, buf.reshape(nb, wb * rpp, dh), new.reshape(nb, t * rpp, dh))
    return o, lse, st.reshape(nb, keep, 2, hkv, dh)


def _page_specs(shape_tail, npg, upg):
    nd = len(shape_tail)
    return [pl.BlockSpec((None, PAGE) + shape_tail,
                         (lambda b, p, pt, u=u: (pt[b * npg + p * upg + u],) + (0,) * (nd + 1)))
            for u in range(upg)]


def _index_rows(qi, w, kic, T):
    r = lax.dot_general(qi, kic, _NT, preferred_element_type=F32)
    x = jnp.maximum(r, 0.0) * w[:, :1]
    return jnp.sum(x.reshape(A_IDX_HEADS, T, x.shape[1]), axis=0) * A_IDX_SCALE


def _dsa_scores_kernel(pt_ref, qi_ref, w_ref, *rest, upg, T):
    pages, o_ref = rest[:upg], rest[upg]
    kic = jnp.concatenate([pg[...].astype(BF16) for pg in pages], axis=0)
    o_ref[...] = _index_rows(qi_ref[...], w_ref[...], kic, T)


def dsa_sample_scores(pt, qi_s, w_s, idxk_pool, *, npg, upg, T):
    nb = qi_s.shape[0]
    rows = A_IDX_HEADS * T
    return pl.pallas_call(
        functools.partial(_dsa_scores_kernel, upg=upg, T=T),
        grid_spec=pltpu.PrefetchScalarGridSpec(
            num_scalar_prefetch=1, grid=(nb, npg // upg),
            in_specs=[pl.BlockSpec((None, rows, A_IDX_DIM), lambda b, p, pt: (b, 0, 0)),
                      pl.BlockSpec((None, rows, LANES), lambda b, p, pt: (b, 0, 0))]
                     + _page_specs((A_IDX_DIM,), npg, upg),
            out_specs=pl.BlockSpec((None, T, upg * PAGE), lambda b, p, pt: (b, 0, p))),
        out_shape=jax.ShapeDtypeStruct((nb, T, npg * PAGE), F32),
        compiler_params=_params(("parallel", "parallel")),
        name="dsa_sample_scores",
    )(pt, qi_s, w_s, *([idxk_pool] * upg))


def _dsa_thr_kernel(sc_ref, qi_ref, w_ref, kin_ref, thr_ref, sel_ref, key_ref, *, sb, T, past, topk, ck):
    rows = sb * T
    key_ref[:, :past] = _float_key(sc_ref[...].reshape(rows, past))
    jn = lax.broadcasted_iota(I32, (T, LANES), 1)
    tn = lax.broadcasted_iota(I32, (T, LANES), 0)
    causal = (jn <= tn) & (jn < T)
    for i in range(sb):
        sn = _index_rows(qi_ref[i], w_ref[i], kin_ref[i], T)
        key_ref[i * T:(i + 1) * T, past:] = _float_key(jnp.where(causal, sn, -jnp.inf))

    def count_ge(cand):
        def inner(c, cnt):
            off = pl.multiple_of(c * ck, ck)
            ge = (key_ref[:, pl.ds(off, ck)] >= cand).astype(I32)
            for u in range(ck // LANES):
                cnt = cnt + ge[:, u * LANES:(u + 1) * LANES]
            return cnt
        cnt = lax.fori_loop(0, past // ck, inner, (key_ref[:, past:] >= cand).astype(I32))
        return jnp.sum(cnt, axis=1, keepdims=True)

    thr = _kth_key(count_ge, rows, topk)
    thr_ref[...] = jnp.broadcast_to(thr, (rows, LANES)).reshape(sb, T, LANES)
    sel = (key_ref[:, past:] >= thr).reshape(sb, T, LANES) & causal[None]
    sel_ref[...] = sel.astype(I32)


def dsa_sample_threshold(scores, qi_s, w_s, ki_new, *, T, sb=16):
    nb, _, past = scores.shape
    sb = _pick(nb, sb)
    rows = A_IDX_HEADS * T
    topk = min(A_TOPK, (past + T) // 4)
    ck = 512
    assert past % ck == 0
    shp = jax.ShapeDtypeStruct((nb, T, LANES), I32)
    return pl.pallas_call(
        functools.partial(_dsa_thr_kernel, sb=sb, T=T, past=past, topk=topk, ck=ck),
        grid=(nb // sb,),
        in_specs=[pl.BlockSpec((sb, T, past), lambda b: (b, 0, 0)),
                  pl.BlockSpec((sb, rows, A_IDX_DIM), lambda b: (b, 0, 0)),
                  pl.BlockSpec((sb, rows, LANES), lambda b: (b, 0, 0)),
                  pl.BlockSpec((sb, LANES, A_IDX_DIM), lambda b: (b, 0, 0))],
        out_specs=[pl.BlockSpec((sb, T, LANES), lambda b: (b, 0, 0))] * 2,
        out_shape=[shp, shp],
        scratch_shapes=[pltpu.VMEM((sb * T, past + LANES), I32)],
        compiler_params=_params(("parallel",)),
        name="dsa_sample_threshold",
    )(scores, qi_s, w_s, ki_new)


def _online_update(m_ref, l_ref, acc_ref, s, pv):
    m = m_ref[...]
    m_new = jnp.maximum(m, jnp.max(s, axis=1, keepdims=True))
    alpha = jnp.exp(m - m_new)
    p = jnp.exp(s - m_new)
    l_ref[...] = alpha * l_ref[...] + jnp.sum(p, axis=1, keepdims=True)
    acc_ref[...] = alpha * acc_ref[...] + pv(p.astype(BF16))
    m_ref[...] = m_new


def _dsa_attn_kernel(pt_ref, q_ref, sc_ref, thr_ref, sel_ref, kn_ref, vn_ref, *rest, upg, T, scale):
    pages = rest[:upg]
    o_ref, qbd_ref, m_ref, l_ref, acc_ref = rest[upg:]
    p = pl.program_id(1)
    kvd = A_KV_HEADS * A_HEAD_DIM
    rows = A_HEADS * T
    rper = rows // A_KV_HEADS

    @pl.when(p == 0)
    def _():
        q = q_ref[...]
        rr = lax.broadcasted_iota(I32, q.shape, 0) // rper
        for hh in range(A_KV_HEADS):
            qbd_ref[:, hh * A_HEAD_DIM:(hh + 1) * A_HEAD_DIM] = jnp.where(rr == hh, q, jnp.zeros_like(q))
        m_ref[...] = jnp.full(m_ref.shape, NEG_BIG, F32)
        l_ref[...] = jnp.zeros(l_ref.shape, F32)
        acc_ref[...] = jnp.zeros(acc_ref.shape, F32)

    qbd = qbd_ref[...]

    def gather(kv):
        rpp = 2 * A_KV_HEADS
        return jnp.concatenate(
            [jnp.concatenate([pg[pl.ds(kv * A_KV_HEADS + hh, PAGE, stride=rpp), :] for hh in range(A_KV_HEADS)],
                             axis=1).astype(BF16)
             for pg in pages], axis=0)

    s = lax.dot_general(qbd, gather(0), _NT, preferred_element_type=F32) * scale
    thr = jnp.concatenate([thr_ref[...]] * upg, axis=1)
    sel = _float_key(sc_ref[...]) >= thr
    s = jnp.where(sel[None], s.reshape(A_HEADS, T, upg * PAGE), -jnp.inf).reshape(rows, upg * PAGE)
    _online_update(m_ref, l_ref, acc_ref, s, lambda pb: jnp.dot(pb, gather(1), preferred_element_type=F32))

    @pl.when(p == pl.num_programs(1) - 1)
    def _():
        sn = lax.dot_general(qbd, kn_ref[...], _NT, preferred_element_type=F32) * scale
        sn = jnp.where((sel_ref[...] > 0)[None], sn.reshape(A_HEADS, T, LANES), -jnp.inf).reshape(rows, LANES)
        _online_update(m_ref, l_ref, acc_ref, sn,
                       lambda pb: jnp.dot(pb, vn_ref[...], preferred_element_type=F32))
        o = acc_ref[...] / l_ref[...]
        for hh in range(A_KV_HEADS):
            o_ref[hh * rper:(hh + 1) * rper, :] = (
                o[hh * rper:(hh + 1) * rper, hh * A_HEAD_DIM:(hh + 1) * A_HEAD_DIM].astype(o_ref.dtype))


def dsa_sample_attn(pt, q_s, scores, thr, sel_new, k_new, v_new, kv_pool, *, npg, upg, T):
    nb = q_s.shape[0]
    rows = A_HEADS * T
    kvd = A_KV_HEADS * A_HEAD_DIM
    per_seq = lambda shape: pl.BlockSpec((None,) + shape, lambda b, p, pt: (b, 0, 0))
    return pl.pallas_call(
        functools.partial(_dsa_attn_kernel, upg=upg, T=T, scale=A_HEAD_DIM ** -0.5),
        grid_spec=pltpu.PrefetchScalarGridSpec(
            num_scalar_prefetch=1, grid=(nb, npg // upg),
            in_specs=[per_seq((rows, A_HEAD_DIM)),
                      pl.BlockSpec((None, T, upg * PAGE), lambda b, p, pt: (b, 0, p)),
                      per_seq((T, LANES)), per_seq((T, LANES)),
                      per_seq((LANES, kvd)), per_seq((LANES, kvd))]
                     + [pl.BlockSpec((None, 2 * A_KV_HEADS * PAGE, A_HEAD_DIM),
                                     lambda b, p, pt, u=u: (pt[b * npg + p * upg + u], 0, 0)) for u in range(upg)],
            out_specs=per_seq((rows, A_HEAD_DIM)),
            scratch_shapes=[pltpu.VMEM((rows, kvd), BF16), pltpu.VMEM((rows, 1), F32),
                            pltpu.VMEM((rows, 1), F32), pltpu.VMEM((rows, kvd), F32)]),
        out_shape=jax.ShapeDtypeStruct((nb, rows, A_HEAD_DIM), BF16),
        compiler_params=_params(("parallel", "arbitrary")),
        name="dsa_sample_attn",
    )(pt, q_s, scores, thr, sel_new, k_new, v_new, *([kv_pool] * upg))


def _mla_sample_kernel(pt_ref, ql_ref, qr_ref, cn_ref, rn_ref, *rest, upg, T):
    lat, rope = rest[:upg], rest[upg:2 * upg]
    o_ref, m_ref, l_ref, acc_ref = rest[2 * upg:]
    p = pl.program_id(1)
    rows = D_HEADS * T

    @pl.when(p == 0)
    def _():
        m_ref[...] = jnp.full(m_ref.shape, NEG_BIG, F32)
        l_ref[...] = jnp.zeros(l_ref.shape, F32)
        acc_ref[...] = jnp.zeros(acc_ref.shape, F32)

    ql, qr = ql_ref[...].astype(BF16), qr_ref[...]

    def scores(c, kr):
        return (lax.dot_general(ql, c, _NT, preferred_element_type=F32)
                + lax.dot_general(qr, kr, _NT, preferred_element_type=F32)) * MLA_SCALE

    c_all = jnp.concatenate([x[...].astype(BF16) for x in lat], axis=0)
    r_all = jnp.concatenate([x[...].astype(BF16) for x in rope], axis=0)
    _online_update(m_ref, l_ref, acc_ref, scores(c_all, r_all),
                   lambda pb: jnp.dot(pb, c_all, preferred_element_type=F32))

    @pl.when(p == pl.num_programs(1) - 1)
    def _():
        cn = cn_ref[...]
        sn = scores(cn, rn_ref[...])
        jn = lax.broadcasted_iota(I32, (T, LANES), 1)
        tn = lax.broadcasted_iota(I32, (T, LANES), 0)
        sn = jnp.where(((jn <= tn) & (jn < T))[None], sn.reshape(D_HEADS, T, LANES), -jnp.inf).reshape(rows, LANES)
        _online_update(m_ref, l_ref, acc_ref, sn, lambda pb: jnp.dot(pb, cn, preferred_element_type=F32))
        o_ref[...] = (acc_ref[...] / l_ref[...]).astype(o_ref.dtype)


def mla_sample_attn(pt, q_lat, q_rope, c_new, r_new, lat_pool, rope_pool, *, npg, upg, T):
    nb = q_lat.shape[0]
    rows = D_HEADS * T
    per_seq = lambda shape: pl.BlockSpec((None,) + shape, lambda b, p, pt: (b, 0, 0))
    return pl.pallas_call(
        functools.partial(_mla_sample_kernel, upg=upg, T=T),
        grid_spec=pltpu.PrefetchScalarGridSpec(
            num_scalar_prefetch=1, grid=(nb, npg // upg),
            in_specs=[per_seq((rows, D_KV_LORA)), per_seq((rows, D_ROPE)),
                      per_seq((LANES, D_KV_LORA)), per_seq((LANES, D_ROPE))]
                     + _page_specs((D_KV_LORA,), npg, upg) + _page_specs((D_ROPE,), npg, upg),
            out_specs=per_seq((rows, D_KV_LORA)),
            scratch_shapes=[pltpu.VMEM((rows, 1), F32), pltpu.VMEM((rows, 1), F32),
                            pltpu.VMEM((rows, D_KV_LORA), F32)]),
        out_shape=jax.ShapeDtypeStruct((nb, rows, D_KV_LORA), F32),
        compiler_params=_params(("parallel", "arbitrary")),
        name="mla_sample_attn",
    )(pt, q_lat, q_rope, c_new, r_new, *([lat_pool] * upg), *([rope_pool] * upg))


def _head_mm_kernel(a_ref, w_ref, o_ref):
    nb, t, ka = a_ref.shape
    out = jnp.dot(a_ref[...].reshape(nb * t, ka).astype(BF16), w_ref[...], preferred_element_type=F32)
    o_ref[...] = out.reshape(nb, t, out.shape[1]).astype(o_ref.dtype)


def head_mm(a, w, T, out_dtype):
    nb, _, ka = a.shape
    nh, _, nbo = w.shape
    return pl.pallas_call(
        _head_mm_kernel,
        grid=(nh,),
        in_specs=[pl.BlockSpec((nb, T, ka), lambda h: (0, h, 0)),
                  pl.BlockSpec((None, ka, nbo), lambda h: (h, 0, 0))],
        out_specs=pl.BlockSpec((nb, T, nbo), lambda h: (0, h, 0)),
        out_shape=jax.ShapeDtypeStruct((nb, nh * T, nbo), out_dtype),
        compiler_params=_params(("parallel",)),
        name="head_mm",
    )(a, w)


def _flags(*runs):
    out = []
    for cnt, flag in runs:
        out += [flag] * cnt
    return jnp.array(out, I32)


def _pad_tokens(a, rows=LANES):
    nb, t, c = a.shape
    return jnp.concatenate([a, jnp.zeros((nb, rows - t, c), a.dtype)], axis=1)


def kernel(x_prompt, x_sample, cache_a_kv, cache_a_idx_k, state_b_kv, state_c_kv_0, state_c_kv_1, state_c_kv_2,
           cache_d_latent, cache_d_k_rope, page_table, norm_mix, norm_ffn, norm_final, a_w_in, a_w_out, b_w_in,
           b_sinks, b_w_out, c_w_in, c_w_out, d_w_in, d_q_norm, d_w_q_up, d_kv_norm, d_w_kv_up, d_w_out,
           ffn_w_in, ffn_w_out):
    B, S, DM = x_prompt.shape
    NB, T, _ = x_sample.shape
    NPG = page_table.shape[1]
    PAST = NPG * PAGE
    MP, MS = B * S, NB * T
    assert T == SUBLANES and S % QBLK == 0 and PAST % 512 == 0
    UPG = _pick(NPG, 8)

    x = jnp.concatenate([x_prompt.reshape(MP, DM), x_sample.reshape(MS, DM)], axis=0)
    pos = jnp.concatenate([jnp.tile(jnp.arange(S), B), jnp.tile(PAST + jnp.arange(T), NB)])
    cos128, sin128 = _rope_tables(pos, 128)
    cos64, sin64 = _rope_tables(pos, 64)
    pt = page_table.reshape(-1).astype(I32)

    def tok_rows(a, heads, dim):
        return a.reshape(NB, T, heads, dim).transpose(0, 2, 1, 3).reshape(NB, heads * T, dim)

    def rows_tok(a, heads, dim):
        return a.reshape(NB, heads, T, dim).transpose(0, 2, 1, 3).reshape(MS, heads * dim)

    def layer_tail(att, w_out, layer, xin):
        y = mm_res(att, w_out.astype(BF16), xin)
        return ffn(y, norm_ffn[layer][None], ffn_w_in[layer].astype(BF16), ffn_w_out[layer].astype(BF16))

    wq, wkv, wqi, wki, wwi = jnp.split(a_w_in, [2048, 3072, 5120, 5248], axis=1)
    wa = jnp.concatenate([wq, wqi, wkv, wki, wwi, jnp.zeros((DM, A_NP - a_w_in.shape[1]), F32)], axis=1)
    p32, p16 = norm_proj(x, 0, norm_mix[0][None], wa.astype(BF16),
                         _flags((16, 1), (16, 1), (4, 1), (4, 0), (1, 1), (1, 0)), cos128, sin128, 128)
    a_kv_p = p32[:MP, A_K0:A_KI0].reshape(B, S, 2, A_KV_HEADS, A_HEAD_DIM)
    a_kv_s = p32[MP:, A_K0:A_KI0].reshape(NB, T, 2, A_KV_HEADS, A_HEAD_DIM)
    a_ik_p = p32[:MP, A_KI0:A_WI0].reshape(B, S, A_IDX_DIM)
    a_ik_s = p32[MP:, A_KI0:A_WI0].reshape(NB, T, A_IDX_DIM)
    att_p = dsa_prompt(p16, p32, B, S)

    s16, s32 = p16[MP:], p32[MP:]
    qi_s = tok_rows(s16[:, A_QI0:A_K0], A_IDX_HEADS, A_IDX_DIM)
    w_s = s32[:, A_WI0:A_WI0 + A_IDX_HEADS].reshape(NB, T, A_IDX_HEADS).transpose(0, 2, 1)
    w_s = jnp.broadcast_to(w_s.reshape(NB, A_IDX_HEADS * T, 1), (NB, A_IDX_HEADS * T, LANES))
    scores = dsa_sample_scores(pt, qi_s, w_s, cache_a_idx_k, npg=NPG, upg=UPG, T=T)
    ki_new = _pad_tokens(s16[:, A_KI0:A_WI0].reshape(NB, T, A_IDX_DIM))
    thr, sel_new = dsa_sample_threshold(scores, qi_s, w_s, ki_new, T=T)
    kvd = A_KV_HEADS * A_HEAD_DIM
    att_s = dsa_sample_attn(
        pt, tok_rows(s16[:, A_Q0:A_QI0], A_HEADS, A_HEAD_DIM), scores, thr, sel_new,
        _pad_tokens(s16[:, A_K0:A_V0].reshape(NB, T, kvd)), _pad_tokens(s16[:, A_V0:A_KI0].reshape(NB, T, kvd)),
        cache_a_kv.reshape(cache_a_kv.shape[0], PAGE * 2 * A_KV_HEADS, A_HEAD_DIM), npg=NPG, upg=UPG, T=T)
    att = jnp.concatenate([att_p, rows_tok(att_s, A_HEADS, A_HEAD_DIM)], axis=0)
    x = layer_tail(att, a_w_out, 0, x)

    hb, db, kb = B_HEADS, B_HEAD_DIM, B_KV_HEADS
    gb = hb // kb
    p32, p16 = norm_proj(x, 0, norm_mix[1][None], b_w_in.astype(BF16), _flags((16, 1), (2, 1), (2, 0)),
                         cos64, sin64, 64)
    q_w, kv_w = hb * db, 2 * kb * db
    att_p = band_attn(p16, p16, p16, nb=B, s=S, d=1, Hkv=kb, G=gb, Dh=db, Dv=db,
                      q_cb=lambda r, h: 0, k_cb=lambda r, h: q_w // (kb * db), v_cb=lambda r, h: q_w // (kb * db) + 1,
                      heads_in_grid=False, ck=QBLK, window=B_WINDOW, scale=db ** -0.5, sinks=b_sinks)
    b_kv_p = p32[:MP, q_w:q_w + kv_w].reshape(B, S, 2, kb, db)[:, S - min(B_WINDOW, S):]
    s16, s32 = p16[MP:], p32[MP:]
    kvn = s32[:, q_w:q_w + kv_w].reshape(NB, T, 2, kb, db)
    q_s = s16[:, :q_w].reshape(NB, T, kb, gb, db).transpose(0, 2, 3, 1, 4).reshape(NB, kb, gb * T, db)
    o_s, _, b_kv_s = win_sample_attn(q_s, state_b_kv, kvn, sb=8, window=B_WINDOW, dil=1, past=PAST,
                                     scale=db ** -0.5, sinks=b_sinks)
    o_s = o_s.reshape(NB, kb, gb, T, db).transpose(0, 3, 1, 2, 4).reshape(MS, q_w)
    att = jnp.concatenate([att_p, o_s.astype(BF16)], axis=0)
    x = layer_tail(att, b_w_out, 1, x)

    hc, dc, kc = C_HEADS, C_HEAD_DIM, C_KV_HEADS
    gc = hc // kc
    gw = (hc + 2 * kc) * dc
    p32, p16 = norm_proj(x, 0, norm_mix[2][None], c_w_in.astype(BF16),
                         _flags(*([(hc, 1), (kc, 1), (kc, 0)] * len(C_GROUPS))), cos128, sin128, 128)
    s16, s32 = p16[MP:], p32[MP:]
    ncol = p16.shape[1]
    outs, lses, c_states = [], [], []
    for gi, ((window, dil), buf) in enumerate(zip(C_GROUPS, (state_c_kv_0, state_c_kv_1, state_c_kv_2))):
        q0, k0 = gi * gw, gi * gw + hc * dc
        v0 = k0 + kc * dc
        o_p, lse_p = band_attn(
            p16, p16, p16, nb=B, s=S, d=dil, Hkv=kc, G=gc, Dh=dc, Dv=dc,
            q_cb=lambda r, h, q0=q0: (r * ncol + q0) // (gc * dc) + h,
            k_cb=lambda r, h, k0=k0: (r * ncol + k0) // dc + h,
            v_cb=lambda r, h, v0=v0: (r * ncol + v0) // dc + h,
            heads_in_grid=True, ck=QBLK, window=window // dil, scale=dc ** -0.5, want_lse=True, out_dtype=F32)
        c_p = p32[:MP, k0:k0 + 2 * kc * dc].reshape(B, S, 2, kc, dc)[:, S - min(window, S):]
        kvn = s32[:, k0:k0 + 2 * kc * dc].reshape(NB, T, 2, kc, dc)
        q_s = s16[:, q0:k0].reshape(NB, T, kc, gc, dc).transpose(0, 2, 3, 1, 4).reshape(NB, kc, gc * T, dc)
        o_s, lse_s, c_s = win_sample_attn(q_s, buf, kvn, sb=8 if buf.shape[1] <= 512 else 2, window=window,
                                          dil=dil, past=PAST, scale=dc ** -0.5)
        untok = lambda a: a.reshape(NB, kc, gc, T, dc).transpose(0, 3, 1, 2, 4).reshape(MS, hc * dc)
        outs.append(jnp.concatenate([o_p, untok(o_s)], axis=0))
        lses.append(jnp.concatenate([lse_p, untok(lse_s)], axis=0))
        c_states += [c_p, c_s]
    x = merge_mm_res(outs, lses, c_w_out.astype(BF16), x)
    x = ffn(x, norm_ffn[2][None], ffn_w_in[2].astype(BF16), ffn_w_out[2].astype(BF16))

    hd = D_HEADS
    dpad = 2 * LANES
    wd = jnp.concatenate([d_w_in, jnp.zeros((DM, LANES - D_ROPE), F32)], axis=1)
    d32, d16 = norm_proj(x, 0, norm_mix[3][None], wd.astype(BF16), _flags((8, 0), (1, 1)), cos64, sin64, 64)
    wqu = d_w_q_up.reshape(D_Q_LORA, hd, D_NOPE + D_ROPE)
    wqu = jnp.concatenate([wqu, jnp.zeros((D_Q_LORA, hd, dpad - D_NOPE - D_ROPE), F32)], axis=2)
    q32, q16 = norm_proj(d32, 0, d_q_norm[None], wqu.reshape(D_Q_LORA, hd * dpad).astype(BF16),
                         _flags(*([(1, 0), (1, 1)] * hd)), cos64, sin64, 64)
    c_kv = rmsnorm(d32, 1, d_kv_norm[None])
    d_lat_p = c_kv[:MP].reshape(B, S, D_KV_LORA)
    d_lat_s = c_kv[MP:].reshape(NB, T, D_KV_LORA)
    kr0 = D_Q_LORA + D_KV_LORA
    d_kr_p = d32[:MP, kr0:kr0 + D_ROPE].reshape(B, S, D_ROPE)
    d_kr_s = d32[MP:, kr0:kr0 + D_ROPE].reshape(NB, T, D_ROPE)
    _, kv16 = norm_proj(d32, 1, d_kv_norm[None], d_w_kv_up.astype(BF16), jnp.zeros((hd * 2,), I32),
                        cos64, sin64, 64, rows=MP)
    att_p = mla_prompt_attn(q16, kv16, d16, kr0 // LANES, B, S)
    qs = q32[MP:].reshape(NB, T, hd, dpad)
    qn_s = qs[..., :D_NOPE].transpose(0, 2, 1, 3).reshape(NB, hd * T, D_NOPE)
    qr_s = qs[..., D_NOPE:D_NOPE + D_ROPE].transpose(0, 2, 1, 3).reshape(NB, hd * T, D_ROPE).astype(BF16)
    w_kv = d_w_kv_up.reshape(D_KV_LORA, hd, D_NOPE + D_V)
    w_uk_t = w_kv[..., :D_NOPE].transpose(1, 2, 0).astype(BF16)
    w_uv = w_kv[..., D_NOPE:].transpose(1, 0, 2).astype(BF16)
    q_lat = head_mm(qn_s, w_uk_t, T, F32)
    o_lat = mla_sample_attn(
        pt, q_lat, qr_s, _pad_tokens(c_kv[MP:].reshape(NB, T, D_KV_LORA)).astype(BF16),
        _pad_tokens(d16[MP:, kr0:kr0 + D_ROPE].reshape(NB, T, D_ROPE)),
        cache_d_latent, cache_d_k_rope, npg=NPG, upg=UPG, T=T)
    o_s = head_mm(o_lat, w_uv, T, F32)
    att = jnp.concatenate([att_p, rows_tok(o_s, hd, D_V).astype(BF16)], axis=0)
    x = layer_tail(att, d_w_out, 3, x)

    y = rmsnorm(x, 0, norm_final[None])
    return (y[:MP].reshape(B, S, DM), y[MP:].reshape(NB, T, DM), a_kv_p, a_kv_s, a_ik_p, a_ik_s, b_kv_p, b_kv_s,
            *c_states, d_lat_p, d_lat_s, d_kr_p, d_kr_s)
```

```python
import functools
import math

import jax
import jax.numpy as jnp
from jax import lax
from jax.experimental import pallas as pl
from jax.experimental.pallas import tpu as pltpu

F32 = jnp.float32
BF16 = jnp.bfloat16
I32 = jnp.int32

LANES = 128
SUBLANES = 8
QBLK = 128
PAGE = 128
NORM_EPS = 1e-6
ROPE_THETA = 10000.0
NEG_BIG = -1e30
VMEM_LIMIT = 56 * 1024 * 1024

A_HEADS, A_HEAD_DIM, A_KV_HEADS, A_IDX_HEADS, A_IDX_DIM, A_TOPK = 16, 128, 4, 16, 128, 256
A_IDX_SCALE = (A_IDX_HEADS * A_IDX_DIM) ** -0.5
B_HEADS, B_HEAD_DIM, B_KV_HEADS, B_WINDOW = 32, 64, 4, 128
C_GROUPS = ((128, 1), (512, 4), (2048, 16))
C_HEADS, C_KV_HEADS, C_HEAD_DIM = 8, 2, 128
D_HEADS, D_NOPE, D_ROPE, D_V, D_Q_LORA, D_KV_LORA = 16, 128, 64, 128, 512, 512
MLA_SCALE = (D_NOPE + D_ROPE) ** -0.5


def _params(sem):
    return pltpu.CompilerParams(dimension_semantics=sem, vmem_limit_bytes=VMEM_LIMIT)


def _pick(total, pref):
    t = min(pref, total)
    while total % t:
        t //= 2
    return t


def _pick_cols(total, pref):
    t = min(pref, total)
    while total % t:
        t -= LANES
    return t


def _rope_tables(pos, dh):
    inv = jnp.power(ROPE_THETA, -jnp.arange(0, dh, 2, dtype=F32) / dh)
    ang = pos.astype(F32)[:, None] * inv[None, :]
    cos, sin = jnp.cos(ang), jnp.sin(ang)
    c = jnp.concatenate([cos, cos], axis=-1)
    s = jnp.concatenate([-sin, sin], axis=-1)
    rep = LANES // dh
    return jnp.tile(c, (1, rep)), jnp.tile(s, (1, rep))


def _rope_block(x, cos, sin, dh):
    half = dh // 2
    if dh == LANES:
        rot = pltpu.roll(x, half, axis=1)
    else:
        lane = lax.broadcasted_iota(I32, x.shape, 1)
        first = (lane % dh) < half
        rot = jnp.where(first, pltpu.roll(x, LANES - half, axis=1), pltpu.roll(x, half, axis=1))
    return x * cos + rot * sin


def _norm_proj_kernel(flags_ref, x_ref, g_ref, w_ref, cos_ref, sin_ref, o32_ref, o16_ref, xn_ref,
                      *, dh, nsub):
    j = pl.program_id(1)

    @pl.when(j == 0)
    def _():
        xf = x_ref[...]
        y = xf * lax.rsqrt(jnp.mean(xf * xf, axis=-1, keepdims=True) + NORM_EPS)
        xn_ref[...] = (y * g_ref[...]).astype(BF16)

    acc = jnp.dot(xn_ref[...], w_ref[...], preferred_element_type=F32)
    for u in range(nsub):
        blk = acc[:, u * LANES:(u + 1) * LANES]
        roped = _rope_block(blk, cos_ref[...], sin_ref[...], dh)
        out = jnp.where(flags_ref[j * nsub + u] > 0, roped, blk)
        o32_ref[:, u * LANES:(u + 1) * LANES] = out
        o16_ref[:, u * LANES:(u + 1) * LANES] = out.astype(BF16)


def norm_proj(x, xcb, g, w, flags, cos, sin, dh, rows=None, tm=1024, tn=768):
    m = x.shape[0] if rows is None else rows
    k, n = w.shape
    tm = _pick(m, tm)
    tn = _pick_cols(n, tn)
    nsub = tn // LANES
    grid_spec = pltpu.PrefetchScalarGridSpec(
        num_scalar_prefetch=1,
        grid=(m // tm, n // tn),
        in_specs=[
            pl.BlockSpec((tm, k), lambda i, j, f: (i, xcb)),
            pl.BlockSpec((1, k), lambda i, j, f: (0, 0)),
            pl.BlockSpec((k, tn), lambda i, j, f: (0, j)),
            pl.BlockSpec((tm, LANES), lambda i, j, f: (i, 0)),
            pl.BlockSpec((tm, LANES), lambda i, j, f: (i, 0)),
        ],
        out_specs=[
            pl.BlockSpec((tm, tn), lambda i, j, f: (i, j)),
            pl.BlockSpec((tm, tn), lambda i, j, f: (i, j)),
        ],
        scratch_shapes=[pltpu.VMEM((tm, k), BF16)],
    )
    return pl.pallas_call(
        functools.partial(_norm_proj_kernel, dh=dh, nsub=nsub),
        grid_spec=grid_spec,
        out_shape=[jax.ShapeDtypeStruct((m, n), F32), jax.ShapeDtypeStruct((m, n), BF16)],
        compiler_params=_params(("parallel", "arbitrary")),
        name="norm_proj",
    )(flags, x, g, w, cos, sin)


def _rmsnorm_kernel(x_ref, g_ref, o_ref):
    xf = x_ref[...]
    y = xf * lax.rsqrt(jnp.mean(xf * xf, axis=-1, keepdims=True) + NORM_EPS)
    o_ref[...] = y * g_ref[...]


def rmsnorm(x, xcb, g, tm=512):
    m = x.shape[0]
    k = g.shape[1]
    tm = _pick(m, tm)
    return pl.pallas_call(
        _rmsnorm_kernel,
        grid=(m // tm,),
        in_specs=[pl.BlockSpec((tm, k), lambda i: (i, xcb)), pl.BlockSpec((1, k), lambda i: (0, 0))],
        out_specs=pl.BlockSpec((tm, k), lambda i: (i, 0)),
        out_shape=jax.ShapeDtypeStruct((m, k), F32),
        compiler_params=_params(("parallel",)),
        name="rmsnorm",
    )(x, g)


def _mm_res_kernel(a_ref, w_ref, r_ref, o_ref):
    o_ref[...] = r_ref[...] + jnp.dot(a_ref[...].astype(BF16), w_ref[...], preferred_element_type=F32)


def mm_res(a, w, res, tm=1024, tn=512):
    m, k = a.shape
    n = w.shape[1]
    tm = _pick(m, tm)
    tn = _pick(n, tn)
    return pl.pallas_call(
        _mm_res_kernel,
        grid=(m // tm, n // tn),
        in_specs=[
            pl.BlockSpec((tm, k), lambda i, j: (i, 0)),
            pl.BlockSpec((k, tn), lambda i, j: (0, j)),
            pl.BlockSpec((tm, tn), lambda i, j: (i, j)),
        ],
        out_specs=pl.BlockSpec((tm, tn), lambda i, j: (i, j)),
        out_shape=jax.ShapeDtypeStruct((m, n), F32),
        compiler_params=_params(("parallel", "parallel")),
        name="mm_res",
    )(a, w, res)


def _merge_mm_res_kernel(o0_ref, o1_ref, o2_ref, l0_ref, l1_ref, l2_ref, w_ref, r_ref, out_ref, mg_ref):
    j = pl.program_id(1)

    @pl.when(j == 0)
    def _():
        l0, l1, l2 = l0_ref[...], l1_ref[...], l2_ref[...]
        mx = jnp.maximum(jnp.maximum(l0, l1), l2)
        e0, e1, e2 = jnp.exp(l0 - mx), jnp.exp(l1 - mx), jnp.exp(l2 - mx)
        den = e0 + e1 + e2
        mg = (e0 / den) * o0_ref[...] + (e1 / den) * o1_ref[...] + (e2 / den) * o2_ref[...]
        mg_ref[...] = mg.astype(BF16)

    out_ref[...] = r_ref[...] + jnp.dot(mg_ref[...], w_ref[...], preferred_element_type=F32)


def merge_mm_res(outs, lses, w, res, tm=256, tn=512):
    m, k = outs[0].shape
    n = w.shape[1]
    tm = _pick(m, tm)
    tn = _pick(n, tn)
    row = pl.BlockSpec((tm, k), lambda i, j: (i, 0))
    return pl.pallas_call(
        _merge_mm_res_kernel,
        grid=(m // tm, n // tn),
        in_specs=[row] * 6 + [
            pl.BlockSpec((k, tn), lambda i, j: (0, j)),
            pl.BlockSpec((tm, tn), lambda i, j: (i, j)),
        ],
        out_specs=pl.BlockSpec((tm, tn), lambda i, j: (i, j)),
        out_shape=jax.ShapeDtypeStruct((m, n), F32),
        scratch_shapes=[pltpu.VMEM((tm, k), BF16)],
        compiler_params=_params(("parallel", "arbitrary")),
        name="merge_mm_res",
    )(*outs, *lses, w, res)


def _ffn_kernel(x_ref, g_ref, wg_ref, wu_ref, wo_ref, o_ref, xn_ref):
    f = pl.program_id(1)

    @pl.when(f == 0)
    def _():
        xf = x_ref[...]
        y = xf * lax.rsqrt(jnp.mean(xf * xf, axis=-1, keepdims=True) + NORM_EPS)
        xn_ref[...] = (y * g_ref[...]).astype(BF16)
        o_ref[...] = xf

    xn = xn_ref[...]
    gate = jnp.dot(xn, wg_ref[...], preferred_element_type=F32)
    up = jnp.dot(xn, wu_ref[...], preferred_element_type=F32)
    h = (gate * jax.nn.sigmoid(gate) * up).astype(BF16)
    o_ref[...] += jnp.dot(h, wo_ref[...], preferred_element_type=F32)


def ffn(x, g, w_in, w_out, layer, tm=1024, tf=512):
    m, d = x.shape
    ff = w_out.shape[1]
    tm = _pick(m, tm)
    tf = _pick_cols(ff, tf)
    nf = ff // tf
    return pl.pallas_call(
        _ffn_kernel,
        grid=(m // tm, nf),
        in_specs=[
            pl.BlockSpec((tm, d), lambda i, f: (i, 0)),
            pl.BlockSpec((1, d), lambda i, f: (0, 0)),
            pl.BlockSpec((None, d, tf), lambda i, f: (layer, 0, f)),
            pl.BlockSpec((None, d, tf), lambda i, f: (layer, 0, f + nf)),
            pl.BlockSpec((None, tf, d), lambda i, f: (layer, f, 0)),
        ],
        out_specs=pl.BlockSpec((tm, d), lambda i, f: (i, 0)),
        out_shape=jax.ShapeDtypeStruct((m, d), F32),
        scratch_shapes=[pltpu.VMEM((tm, d), BF16)],
        compiler_params=_params(("parallel", "arbitrary")),
        name="ffn",
    )(x, g, w_in, w_in, w_out)


_NT = (((1,), (1,)), ((), ()))


def _band_attn_kernel(sink_ref, q_ref, k_ref, v_ref, o_ref, *lse_refs, G, Dh, Dv, nheads, ck, window, scale,
                      has_sink, heads_in_grid):
    qb = pl.program_id(3 if heads_in_grid else 2)
    head0 = pl.program_id(2) if heads_in_grid else 0
    rows = G * QBLK
    q0 = qb * QBLK
    nk = min(QBLK + window, k_ref.shape[0])
    off = pl.multiple_of(jnp.maximum(q0 + QBLK - nk, 0), QBLK)
    dist = (q0 + lax.broadcasted_iota(I32, (QBLK, nk), 0)) - (off + lax.broadcasted_iota(I32, (QBLK, nk), 1))
    valid = (dist >= 0) & (dist <= window)
    for hh in range(nheads):
        qs = jnp.concatenate([q_ref[:, (hh * G + g) * Dh:(hh * G + g + 1) * Dh] for g in range(G)], axis=0)
        s = lax.dot_general(qs, k_ref[pl.ds(off, nk), hh * Dh:(hh + 1) * Dh], _NT,
                            preferred_element_type=F32) * scale
        s = jnp.where(valid[None], s.reshape(G, QBLK, nk), -jnp.inf).reshape(rows, nk)
        m = jnp.max(s, axis=1, keepdims=True)
        if has_sink:
            sink = jnp.concatenate(
                [jnp.full((QBLK, 1), sink_ref[(head0 + hh) * G + g], F32) for g in range(G)], axis=0)
            m = jnp.maximum(m, sink)
        p = jnp.exp(s - m)
        l = jnp.sum(p, axis=1, keepdims=True)
        if has_sink:
            l = l + jnp.exp(sink - m)
        o = jnp.dot(p.astype(BF16), v_ref[pl.ds(off, nk), hh * Dv:(hh + 1) * Dv], preferred_element_type=F32) / l
        lse = m + jnp.log(l)
        for g in range(G):
            lo = (hh * G + g) * Dv
            o_ref[:, lo:lo + Dv] = o[g * QBLK:(g + 1) * QBLK].astype(o_ref.dtype)
            if lse_refs:
                lse_refs[0][:, lo:lo + Dv] = jnp.broadcast_to(lse[g * QBLK:(g + 1) * QBLK], (QBLK, Dv))


def band_attn(qarr, karr, varr, *, nb, s, d, Hkv, G, Dh, Dv, q_cb, k_cb, v_cb, heads_in_grid, ck, window,
              scale, sinks=None, want_lse=False, out_dtype=BF16):
    s_sub = s // d
    nqb = s_sub // QBLK
    nheads = 1 if heads_in_grid else Hkv
    hq = Hkv * G

    def view(a):
        return a.reshape(a.shape[0] // d, d * a.shape[1])

    if heads_in_grid:
        grid = (nb, d, Hkv, nqb)
        qmap = lambda b, r, h, i, sk: (b * nqb + i, q_cb(r, h))
        kmap = lambda b, r, h, i, sk: (b, k_cb(r, h))
        vmap = lambda b, r, h, i, sk: (b, v_cb(r, h))
        omap = lambda b, r, h, i, sk: (b * nqb + i, r * Hkv + h)
        sem = ("parallel", "parallel", "parallel", "parallel")
    else:
        grid = (nb, d, nqb)
        qmap = lambda b, r, i, sk: (b * nqb + i, q_cb(r, 0))
        kmap = lambda b, r, i, sk: (b, k_cb(r, 0))
        vmap = lambda b, r, i, sk: (b, v_cb(r, 0))
        omap = lambda b, r, i, sk: (b * nqb + i, r)
        sem = ("parallel", "parallel", "parallel")

    in_specs = [
        pl.BlockSpec((QBLK, nheads * G * Dh), qmap),
        pl.BlockSpec((s_sub, nheads * Dh), kmap),
        pl.BlockSpec((s_sub, nheads * Dv), vmap),
    ]
    out_specs = [pl.BlockSpec((QBLK, nheads * G * Dv), omap)]
    out_shape = [jax.ShapeDtypeStruct((nb * s_sub, d * hq * Dv), out_dtype)]
    if want_lse:
        out_specs.append(pl.BlockSpec((QBLK, nheads * G * Dv), omap))
        out_shape.append(jax.ShapeDtypeStruct((nb * s_sub, d * hq * Dv), F32))
    has_sink = sinks is not None
    if sinks is None:
        sinks = jnp.zeros((hq,), F32)
    res = pl.pallas_call(
        functools.partial(_band_attn_kernel, G=G, Dh=Dh, Dv=Dv, nheads=nheads, ck=ck, window=window,
                          scale=scale, has_sink=has_sink, heads_in_grid=heads_in_grid),
        grid_spec=pltpu.PrefetchScalarGridSpec(
            num_scalar_prefetch=1, grid=grid, in_specs=in_specs, out_specs=out_specs),
        out_shape=out_shape,
        compiler_params=_params(sem),
        name="band_attn",
    )(sinks.astype(F32), view(qarr), view(karr), view(varr))
    res = [a.reshape(nb * s, hq * Dv) for a in res]
    return res if want_lse else res[0]


def _swa_prompt_kernel(sink_ref, q_ref, k_ref, v_ref, o_ref, kbd_ref, vbd_ref, *, scale):
    i = pl.program_id(1)
    s_len = k_ref.shape[0]
    nh, dh = B_KV_HEADS, B_HEAD_DIM
    npair = B_HEADS // nh // 2
    ck = QBLK

    @pl.when(i == 0)
    def _():
        lane = lax.broadcasted_iota(I32, (s_len, LANES), 1)
        for src, dst in ((k_ref, kbd_ref), (v_ref, vbd_ref)):
            for h in range(nh):
                x = src[:, (h // 2) * LANES:(h // 2 + 1) * LANES].astype(F32)
                r = pltpu.roll(x, dh, axis=1)
                lo, hi = (x, r) if h % 2 == 0 else (r, x)
                dst[h, 0] = jnp.where(lane < dh, lo, 0.0).astype(BF16)
                dst[h, 1] = jnp.where(lane >= dh, hi, 0.0).astype(BF16)

    nk = QBLK + B_WINDOW
    q0 = i * QBLK
    off = pl.multiple_of(jnp.maximum(q0 - B_WINDOW, 0), QBLK)
    dist = (q0 + lax.broadcasted_iota(I32, (QBLK, nk), 0)) - (off + lax.broadcasted_iota(I32, (QBLK, nk), 1))
    valid = (dist >= 0) & (dist <= B_WINDOW)
    valid2 = jnp.concatenate([valid, valid], axis=1)
    rows = npair * QBLK
    lane_lo = lax.broadcasted_iota(I32, (rows, LANES), 1) < dh
    for h in range(nh):
        qs = jnp.concatenate([q_ref[:, (h * npair + j) * LANES:(h * npair + j + 1) * LANES] for j in range(npair)],
                             axis=0)
        kbd = jnp.concatenate([kbd_ref[h, 0, pl.ds(off, nk), :], kbd_ref[h, 1, pl.ds(off, nk), :]], axis=0)
        vbd = jnp.concatenate([vbd_ref[h, 0, pl.ds(off, nk), :], vbd_ref[h, 1, pl.ds(off, nk), :]], axis=0)
        s = lax.dot_general(qs, kbd, _NT, preferred_element_type=F32) * scale
        s = jnp.where(valid2[None], s.reshape(npair, QBLK, 2 * nk), -jnp.inf).reshape(rows, 2 * nk)
        den = []
        ps = []
        for par in range(2):
            sink = jnp.concatenate(
                [jnp.full((QBLK, 1), sink_ref[(h * npair + j) * 2 + par], F32) for j in range(npair)], axis=0)
            sp = s[:, par * nk:(par + 1) * nk]
            m = jnp.maximum(jnp.max(sp, axis=1, keepdims=True), sink)
            e = jnp.exp(sp - m)
            ps.append(e)
            den.append(jnp.sum(e, axis=1, keepdims=True) + jnp.exp(sink - m))
        pv = jnp.dot(jnp.concatenate(ps, axis=1).astype(BF16), vbd, preferred_element_type=F32)
        o = pv / jnp.where(lane_lo, den[0], den[1])
        for j in range(npair):
            o_ref[:, (h * npair + j) * LANES:(h * npair + j + 1) * LANES] = (
                o[j * QBLK:(j + 1) * QBLK].astype(o_ref.dtype))


def swa_prompt_attn(p16, sinks, q_w, nb, s):
    kv_w = B_KV_HEADS * B_HEAD_DIM
    nqb = s // QBLK
    assert q_w % kv_w == 0 and B_WINDOW % QBLK == 0
    return pl.pallas_call(
        functools.partial(_swa_prompt_kernel, scale=B_HEAD_DIM ** -0.5),
        grid_spec=pltpu.PrefetchScalarGridSpec(
            num_scalar_prefetch=1, grid=(nb, nqb),
            in_specs=[pl.BlockSpec((QBLK, q_w), lambda b, i, sk: (b * nqb + i, 0)),
                      pl.BlockSpec((s, kv_w), lambda b, i, sk: (b, q_w // kv_w)),
                      pl.BlockSpec((s, kv_w), lambda b, i, sk: (b, q_w // kv_w + 1))],
            out_specs=pl.BlockSpec((QBLK, q_w), lambda b, i, sk: (b * nqb + i, 0)),
            scratch_shapes=[pltpu.VMEM((B_KV_HEADS, 2, s, LANES), BF16)] * 2),
        out_shape=jax.ShapeDtypeStruct((nb * s, q_w), BF16),
        compiler_params=_params(("parallel", "arbitrary")),
        name="swa_prompt_attn",
    )(sinks.astype(F32), p16, p16, p16)


MLA_QW = 2 * LANES
MLA_QROWS = 256
MLA_CK = 512
MLA_HS = 2


def _mla_prompt_kernel(q_ref, kv_ref, k2_ref, o_ref, kcat_ref, *, hs, qrows, ck):
    i = pl.program_id(2)

    @pl.when(i == 0)
    def _():
        for h in range(hs):
            kcat_ref[h, :, :D_NOPE] = kv_ref[:, h * MLA_QW:h * MLA_QW + D_NOPE]
            kcat_ref[h, :, D_NOPE:] = k2_ref[...]

    q0 = i * qrows
    n_full = q0 // ck
    row = lax.broadcasted_iota(I32, (qrows, ck), 0)
    col = lax.broadcasted_iota(I32, (qrows, ck), 1)

    def chunk(c, carry, masked):
        off = pl.multiple_of(c * ck, ck)
        out = []
        for h in range(hs):
            m, l, acc = carry[h]
            s = lax.dot_general(q_ref[:, h * MLA_QW:(h + 1) * MLA_QW], kcat_ref[h, pl.ds(off, ck), :], _NT,
                                preferred_element_type=F32) * MLA_SCALE
            if masked:
                s = jnp.where((off + col) <= (q0 + row), s, -jnp.inf)
            m_new = jnp.maximum(m, jnp.max(s, axis=1, keepdims=True))
            alpha = jnp.exp(m - m_new)
            p = jnp.exp(s - m_new)
            l = alpha * l + jnp.sum(p, axis=1, keepdims=True)
            v = kv_ref[pl.ds(off, ck), h * MLA_QW + D_NOPE:(h + 1) * MLA_QW]
            acc = alpha * acc + jnp.dot(p.astype(BF16), v, preferred_element_type=F32)
            out.append((m_new, l, acc))
        return tuple(out)

    init = tuple((jnp.full((qrows, 1), NEG_BIG, F32), jnp.zeros((qrows, 1), F32), jnp.zeros((qrows, D_V), F32))
                 for _ in range(hs))
    carry = lax.fori_loop(0, n_full, lambda c, cr: chunk(c, cr, False), init)
    carry = chunk(n_full, carry, True)
    for h in range(hs):
        _, l, acc = carry[h]
        o_ref[:, h * D_V:(h + 1) * D_V] = (acc / l).astype(o_ref.dtype)


def mla_prompt_attn(q16, kv16, d16, kr_cb, nb, s):
    qrows, ck, hs = min(MLA_QROWS, s), min(MLA_CK, s), MLA_HS
    assert ck % qrows == 0 and s % ck == 0 and D_HEADS % hs == 0 and D_NOPE == D_V == LANES
    nq = s // qrows
    return pl.pallas_call(
        functools.partial(_mla_prompt_kernel, hs=hs, qrows=qrows, ck=ck),
        grid=(nb, D_HEADS // hs, nq),
        in_specs=[pl.BlockSpec((qrows, hs * MLA_QW), lambda b, h, i: (b * nq + i, h)),
                  pl.BlockSpec((s, hs * MLA_QW), lambda b, h, i: (b, h)),
                  pl.BlockSpec((s, LANES), lambda b, h, i: (b, kr_cb))],
        out_specs=pl.BlockSpec((qrows, hs * D_V), lambda b, h, i: (b * nq + i, h)),
        out_shape=jax.ShapeDtypeStruct((nb * s, D_HEADS * D_V), BF16),
        scratch_shapes=[pltpu.VMEM((hs, s, MLA_QW), BF16)],
        compiler_params=_params(("parallel", "parallel", "arbitrary")),
        name="mla_prompt_attn",
    )(q16, kv16, d16)


INT_MIN = -2 ** 31


def _float_key(x):
    i = pltpu.bitcast(x + 0.0, I32)
    return i ^ ((i >> 31) & 0x7FFFFFFF)


def _kth_key(count_ge, rows, k):
    def body(it, t):
        bit = 31 - it
        cand = t + lax.shift_left(jnp.int32(1), bit)
        return jnp.where(count_ge(cand) >= k, cand, t)

    return lax.fori_loop(0, 32, body, jnp.full((rows, 1), INT_MIN, I32))


def _dsa_prompt_kernel(q_ref, qi_ref, wi_ref, ki_ref, k_ref, v_ref, o_ref, key_ref, *, ck, topk, scale):
    qb = pl.program_id(1)
    q0 = qb * QBLK
    nk = (q0 + QBLK - 1) // ck + 1
    row = lax.broadcasted_iota(I32, (QBLK, ck), 0)
    col = lax.broadcasted_iota(I32, (QBLK, ck), 1)
    w = wi_ref[...]
    g = A_HEADS // A_KV_HEADS
    dh = A_HEAD_DIM

    def score_chunk(c, carry):
        off = pl.multiple_of(c * ck, ck)
        kic = ki_ref[pl.ds(off, ck), :]
        acc = jnp.zeros((QBLK, ck), F32)
        for h in range(A_IDX_HEADS):
            r = lax.dot_general(qi_ref[:, h * A_IDX_DIM:(h + 1) * A_IDX_DIM], kic, _NT,
                                preferred_element_type=F32)
            acc = acc + jnp.maximum(r, 0.0) * w[:, h:h + 1]
        sc = jnp.where((off + col) <= (q0 + row), acc * A_IDX_SCALE, -jnp.inf)
        key_ref[:, pl.ds(off, ck)] = _float_key(sc)
        return carry

    lax.fori_loop(0, nk, score_chunk, 0)

    def count_ge(cand):
        def inner(c, cnt):
            off = pl.multiple_of(c * ck, ck)
            ge = (key_ref[:, pl.ds(off, ck)] >= cand).astype(I32)
            for u in range(ck // LANES):
                cnt = cnt + ge[:, u * LANES:(u + 1) * LANES]
            return cnt
        cnt = lax.fori_loop(0, nk, inner, jnp.zeros((QBLK, LANES), I32))
        return jnp.sum(cnt, axis=1, keepdims=True)

    thr = _kth_key(count_ge, QBLK, topk)
    rows = g * QBLK
    qs = [jnp.concatenate([q_ref[:, (hh * g + j) * dh:(hh * g + j + 1) * dh] for j in range(g)], axis=0)
          for hh in range(A_KV_HEADS)]

    def step(c, carry):
        off = pl.multiple_of(c * ck, ck)
        sel = (key_ref[:, pl.ds(off, ck)] >= thr) & ((off + col) <= (q0 + row))
        out = []
        for hh in range(A_KV_HEADS):
            m, l, acc = carry[hh]
            s = lax.dot_general(qs[hh], k_ref[pl.ds(off, ck), hh * dh:(hh + 1) * dh], _NT,
                                preferred_element_type=F32) * scale
            s = jnp.where(sel[None], s.reshape(g, QBLK, ck), -jnp.inf).reshape(rows, ck)
            m_new = jnp.maximum(m, jnp.max(s, axis=1, keepdims=True))
            alpha = jnp.exp(m - m_new)
            p = jnp.exp(s - m_new)
            l = alpha * l + jnp.sum(p, axis=1, keepdims=True)
            acc = alpha * acc + jnp.dot(p.astype(BF16), v_ref[pl.ds(off, ck), hh * dh:(hh + 1) * dh],
                                        preferred_element_type=F32)
            out.append((m_new, l, acc))
        return tuple(out)

    init = tuple((jnp.full((rows, 1), NEG_BIG, F32), jnp.zeros((rows, 1), F32), jnp.zeros((rows, dh), F32))
                 for _ in range(A_KV_HEADS))
    res = lax.fori_loop(0, nk, step, init)
    for hh in range(A_KV_HEADS):
        _, l, acc = res[hh]
        o = acc / l
        for j in range(g):
            lo = (hh * g + j) * dh
            o_ref[:, lo:lo + dh] = o[j * QBLK:(j + 1) * QBLK].astype(o_ref.dtype)


A_Q0, A_QI0, A_K0, A_V0, A_KI0, A_WI0, A_NP = 0, 2048, 4096, 4608, 5120, 5248, 5376


def dsa_prompt(p16, p32, nb, s):
    nqb = s // QBLK
    ck = min(512, s)
    topk = min(A_TOPK, s // 4)
    hd = A_HEADS * A_HEAD_DIM
    kvd = A_KV_HEADS * A_HEAD_DIM
    return pl.pallas_call(
        functools.partial(_dsa_prompt_kernel, ck=ck, topk=topk, scale=A_HEAD_DIM ** -0.5),
        grid=(nb, nqb),
        in_specs=[
            pl.BlockSpec((QBLK, hd), lambda b, i: (b * nqb + i, A_Q0 // hd)),
            pl.BlockSpec((QBLK, hd), lambda b, i: (b * nqb + i, A_QI0 // hd)),
            pl.BlockSpec((QBLK, LANES), lambda b, i: (b * nqb + i, A_WI0 // LANES)),
            pl.BlockSpec((s, A_IDX_DIM), lambda b, i: (b, A_KI0 // A_IDX_DIM)),
            pl.BlockSpec((s, kvd), lambda b, i: (b, A_K0 // kvd)),
            pl.BlockSpec((s, kvd), lambda b, i: (b, A_V0 // kvd)),
        ],
        out_specs=pl.BlockSpec((QBLK, hd), lambda b, i: (b * nqb + i, 0)),
        out_shape=jax.ShapeDtypeStruct((nb * s, hd), BF16),
        scratch_shapes=[pltpu.VMEM((QBLK, s), I32)],
        compiler_params=_params(("parallel", "arbitrary")),
        name="dsa_prompt",
    )(p16, p16, p32, p16, p16, p16)


def _pad_rows(a, rows):
    return jnp.concatenate([a, jnp.zeros((rows - a.shape[0], a.shape[1]), a.dtype)], axis=0)


def _win_sample_kernel(sink_ref, q_ref, buf_ref, new_ref, o_ref, lse_ref, st_ref, *, sb, Hkv, G, T, wb, off,
                       window, dil, scale, has_sink):
    rows = G * T
    ncol = wb + LANES
    col = lax.broadcasted_iota(I32, (T, ncol), 1)
    t = lax.broadcasted_iota(I32, (T, ncol), 0)
    dist = wb + t - col
    valid = (dist >= 0) & (dist <= window) & ((dist & (dil - 1)) == 0)
    rpp = 2 * Hkv
    for i in range(sb):
        for h in range(Hkv):
            q = q_ref[i, h]
            kb = buf_ref[i, pl.ds(h, wb, stride=rpp), :].astype(BF16)
            vb = buf_ref[i, pl.ds(Hkv + h, wb, stride=rpp), :].astype(BF16)
            kn = _pad_rows(new_ref[i, pl.ds(h, T, stride=rpp), :], LANES).astype(BF16)
            vn = _pad_rows(new_ref[i, pl.ds(Hkv + h, T, stride=rpp), :], LANES).astype(BF16)
            s = jnp.concatenate([lax.dot_general(q, kb, _NT, preferred_element_type=F32),
                                 lax.dot_general(q, kn, _NT, preferred_element_type=F32)], axis=1) * scale
            s = jnp.where(valid[None], s.reshape(G, T, ncol), -jnp.inf).reshape(rows, ncol)
            m = jnp.max(s, axis=1, keepdims=True)
            if has_sink:
                sink = jnp.concatenate([jnp.full((T, 1), sink_ref[h * G + g], F32) for g in range(G)], axis=0)
                m = jnp.maximum(m, sink)
            e = jnp.exp(s - m)
            den = jnp.sum(e, axis=1, keepdims=True)
            if has_sink:
                den = den + jnp.exp(sink - m)
            eb = e.astype(BF16)
            acc = (jnp.dot(eb[:, :wb], vb, preferred_element_type=F32)
                   + jnp.dot(eb[:, wb:], vn, preferred_element_type=F32))
            o_ref[i, h] = acc / den
            lse_ref[i, h] = jnp.broadcast_to(m + jnp.log(den), (rows, o_ref.shape[-1]))
        st_ref[i, pl.ds(0, (wb - off) * rpp), :] = buf_ref[i, pl.ds(off * rpp, (wb - off) * rpp), :]
        st_ref[i, pl.ds((wb - off) * rpp, T * rpp), :] = new_ref[i]


def win_sample_attn(q, buf, new, *, sb, window, dil, past, scale, sinks=None):
    nb, hkv, rows, dh = q.shape
    wb, t = buf.shape[1], new.shape[1]
    g = rows // t
    rpp = 2 * hkv
    keep = min(window, past + t)
    off = wb + t - keep
    sb = _pick(nb, sb)
    has_sink = sinks is not None
    if sinks is None:
        sinks = jnp.zeros((hkv * g,), F32)
    qspec = pl.BlockSpec((sb, hkv, rows, dh), lambda b, sk: (b, 0, 0, 0))
    oshape = jax.ShapeDtypeStruct((nb, hkv, rows, dh), F32)
    flat = lambda n: pl.BlockSpec((sb, n * rpp, dh), lambda b, sk: (b, 0, 0))
    o, lse, st = pl.pallas_call(
        functools.partial(_win_sample_kernel, sb=sb, Hkv=hkv, G=g, T=t, wb=wb, off=off, window=window, dil=dil,
                          scale=scale, has_sink=has_sink),
        grid_spec=pltpu.PrefetchScalarGridSpec(
            num_scalar_prefetch=1, grid=(nb // sb,),
            in_specs=[qspec, flat(wb), flat(t)],
            out_specs=[qspec, qspec, flat(keep)]),
        out_shape=[oshape, oshape, jax.ShapeDtypeStruct((nb, keep * rpp, dh), F32)],
        compiler_params=_params(("parallel",)),
        name="win_sample_attn",
    )(sinks.astype(F32), q, buf.reshape(nb, wb * rpp, dh), new.reshape(nb, t * rpp, dh))
    return o, lse, st.reshape(nb, keep, 2, hkv, dh)


def _page_specs(shape_tail, npg, upg):
    nd = len(shape_tail)
    return [pl.BlockSpec((None, PAGE) + shape_tail,
                         (lambda b, p, pt, u=u: (pt[b * npg + p * upg + u],) + (0,) * (nd + 1)))
            for u in range(upg)]


def _index_rows(qi, w, kic, T):
    r = lax.dot_general(qi, kic, _NT, preferred_element_type=F32)
    x = jnp.maximum(r, 0.0) * w[:, :1]
    return jnp.sum(x.reshape(A_IDX_HEADS, T, x.shape[1]), axis=0) * A_IDX_SCALE


def _dsa_scores_kernel(pt_ref, qi_ref, w_ref, *rest, upg, T):
    pages, o_ref = rest[:upg], rest[upg]
    kic = jnp.concatenate([pg[...].astype(BF16) for pg in pages], axis=0)
    o_ref[...] = _index_rows(qi_ref[...], w_ref[...], kic, T)


def dsa_sample_scores(pt, qi_s, w_s, idxk_pool, *, npg, upg, T):
    nb = qi_s.shape[0]
    rows = A_IDX_HEADS * T
    return pl.pallas_call(
        functools.partial(_dsa_scores_kernel, upg=upg, T=T),
        grid_spec=pltpu.PrefetchScalarGridSpec(
            num_scalar_prefetch=1, grid=(nb, npg // upg),
            in_specs=[pl.BlockSpec((None, rows, A_IDX_DIM), lambda b, p, pt: (b, 0, 0)),
                      pl.BlockSpec((None, rows, LANES), lambda b, p, pt: (b, 0, 0))]
                     + _page_specs((A_IDX_DIM,), npg, upg),
            out_specs=pl.BlockSpec((None, T, upg * PAGE), lambda b, p, pt: (b, 0, p))),
        out_shape=jax.ShapeDtypeStruct((nb, T, npg * PAGE), F32),
        compiler_params=_params(("parallel", "parallel")),
        name="dsa_sample_scores",
    )(pt, qi_s, w_s, *([idxk_pool] * upg))


def _dsa_thr_kernel(sc_ref, qi_ref, w_ref, kin_ref, thr_ref, sel_ref, key_ref, *, sb, T, past, topk, ck):
    rows = sb * T
    key_ref[:, :past] = _float_key(sc_ref[...].reshape(rows, past))
    jn = lax.broadcasted_iota(I32, (T, LANES), 1)
    tn = lax.broadcasted_iota(I32, (T, LANES), 0)
    causal = (jn <= tn) & (jn < T)
    for i in range(sb):
        sn = _index_rows(qi_ref[i], w_ref[i], kin_ref[i], T)
        key_ref[i * T:(i + 1) * T, past:] = _float_key(jnp.where(causal, sn, -jnp.inf))

    def count_ge(cand):
        def inner(c, cnt):
            off = pl.multiple_of(c * ck, ck)
            ge = (key_ref[:, pl.ds(off, ck)] >= cand).astype(I32)
            for u in range(ck // LANES):
                cnt = cnt + ge[:, u * LANES:(u + 1) * LANES]
            return cnt
        cnt = lax.fori_loop(0, past // ck, inner, (key_ref[:, past:] >= cand).astype(I32))
        return jnp.sum(cnt, axis=1, keepdims=True)

    thr = _kth_key(count_ge, rows, topk)
    thr_ref[...] = jnp.broadcast_to(thr, (rows, LANES)).reshape(sb, T, LANES)
    sel = (key_ref[:, past:] >= thr).reshape(sb, T, LANES) & causal[None]
    sel_ref[...] = sel.astype(I32)


def dsa_sample_threshold(scores, qi_s, w_s, ki_new, *, T, sb=16):
    nb, _, past = scores.shape
    sb = _pick(nb, sb)
    rows = A_IDX_HEADS * T
    topk = min(A_TOPK, (past + T) // 4)
    ck = 512
    assert past % ck == 0
    shp = jax.ShapeDtypeStruct((nb, T, LANES), I32)
    return pl.pallas_call(
        functools.partial(_dsa_thr_kernel, sb=sb, T=T, past=past, topk=topk, ck=ck),
        grid=(nb // sb,),
        in_specs=[pl.BlockSpec((sb, T, past), lambda b: (b, 0, 0)),
                  pl.BlockSpec((sb, rows, A_IDX_DIM), lambda b: (b, 0, 0)),
                  pl.BlockSpec((sb, rows, LANES), lambda b: (b, 0, 0)),
                  pl.BlockSpec((sb, LANES, A_IDX_DIM), lambda b: (b, 0, 0))],
        out_specs=[pl.BlockSpec((sb, T, LANES), lambda b: (b, 0, 0))] * 2,
        out_shape=[shp, shp],
        scratch_shapes=[pltpu.VMEM((sb * T, past + LANES), I32)],
        compiler_params=_params(("parallel",)),
        name="dsa_sample_threshold",
    )(scores, qi_s, w_s, ki_new)


def _online_update(m_ref, l_ref, acc_ref, s, pv):
    m = m_ref[...]
    m_new = jnp.maximum(m, jnp.max(s, axis=1, keepdims=True))
    alpha = jnp.exp(m - m_new)
    p = jnp.exp(s - m_new)
    l_ref[...] = alpha * l_ref[...] + jnp.sum(p, axis=1, keepdims=True)
    acc_ref[...] = alpha * acc_ref[...] + pv(p.astype(BF16))
    m_ref[...] = m_new


def _dsa_attn_kernel(pt_ref, q_ref, sc_ref, thr_ref, sel_ref, kn_ref, vn_ref, *rest, upg, T, scale):
    pages = rest[:upg]
    o_ref, qbd_ref, m_ref, l_ref, acc_ref = rest[upg:]
    p = pl.program_id(1)
    kvd = A_KV_HEADS * A_HEAD_DIM
    rows = A_HEADS * T
    rper = rows // A_KV_HEADS

    @pl.when(p == 0)
    def _():
        q = q_ref[...]
        rr = lax.broadcasted_iota(I32, q.shape, 0) // rper
        for hh in range(A_KV_HEADS):
            qbd_ref[:, hh * A_HEAD_DIM:(hh + 1) * A_HEAD_DIM] = jnp.where(rr == hh, q, jnp.zeros_like(q))
        m_ref[...] = jnp.full(m_ref.shape, NEG_BIG, F32)
        l_ref[...] = jnp.zeros(l_ref.shape, F32)
        acc_ref[...] = jnp.zeros(acc_ref.shape, F32)

    qbd = qbd_ref[...]

    def gather(kv):
        rpp = 2 * A_KV_HEADS
        return jnp.concatenate(
            [jnp.concatenate([pg[pl.ds(kv * A_KV_HEADS + hh, PAGE, stride=rpp), :] for hh in range(A_KV_HEADS)],
                             axis=1).astype(BF16)
             for pg in pages], axis=0)

    s = lax.dot_general(qbd, gather(0), _NT, preferred_element_type=F32) * scale
    thr = jnp.concatenate([thr_ref[...]] * upg, axis=1)
    sel = _float_key(sc_ref[...]) >= thr
    s = jnp.where(sel[None], s.reshape(A_HEADS, T, upg * PAGE), -jnp.inf).reshape(rows, upg * PAGE)
    _online_update(m_ref, l_ref, acc_ref, s, lambda pb: jnp.dot(pb, gather(1), preferred_element_type=F32))

    @pl.when(p == pl.num_programs(1) - 1)
    def _():
        sn = lax.dot_general(qbd, kn_ref[...], _NT, preferred_element_type=F32) * scale
        sn = jnp.where((sel_ref[...] > 0)[None], sn.reshape(A_HEADS, T, LANES), -jnp.inf).reshape(rows, LANES)
        _online_update(m_ref, l_ref, acc_ref, sn,
                       lambda pb: jnp.dot(pb, vn_ref[...], preferred_element_type=F32))
        o = acc_ref[...] / l_ref[...]
        for hh in range(A_KV_HEADS):
            o_ref[hh * rper:(hh + 1) * rper, :] = (
                o[hh * rper:(hh + 1) * rper, hh * A_HEAD_DIM:(hh + 1) * A_HEAD_DIM].astype(o_ref.dtype))


def dsa_sample_attn(pt, q_s, scores, thr, sel_new, k_new, v_new, kv_pool, *, npg, upg, T):
    nb = q_s.shape[0]
    rows = A_HEADS * T
    kvd = A_KV_HEADS * A_HEAD_DIM
    per_seq = lambda shape: pl.BlockSpec((None,) + shape, lambda b, p, pt: (b, 0, 0))
    return pl.pallas_call(
        functools.partial(_dsa_attn_kernel, upg=upg, T=T, scale=A_HEAD_DIM ** -0.5),
        grid_spec=pltpu.PrefetchScalarGridSpec(
            num_scalar_prefetch=1, grid=(nb, npg // upg),
            in_specs=[per_seq((rows, A_HEAD_DIM)),
                      pl.BlockSpec((None, T, upg * PAGE), lambda b, p, pt: (b, 0, p)),
                      per_seq((T, LANES)), per_seq((T, LANES)),
                      per_seq((LANES, kvd)), per_seq((LANES, kvd))]
                     + [pl.BlockSpec((None, 2 * A_KV_HEADS * PAGE, A_HEAD_DIM),
                                     lambda b, p, pt, u=u: (pt[b * npg + p * upg + u], 0, 0)) for u in range(upg)],
            out_specs=per_seq((rows, A_HEAD_DIM)),
            scratch_shapes=[pltpu.VMEM((rows, kvd), BF16), pltpu.VMEM((rows, 1), F32),
                            pltpu.VMEM((rows, 1), F32), pltpu.VMEM((rows, kvd), F32)]),
        out_shape=jax.ShapeDtypeStruct((nb, rows, A_HEAD_DIM), BF16),
        compiler_params=_params(("parallel", "arbitrary")),
        name="dsa_sample_attn",
    )(pt, q_s, scores, thr, sel_new, k_new, v_new, *([kv_pool] * upg))


def _mla_sample_kernel(pt_ref, ql_ref, qr_ref, cn_ref, rn_ref, *rest, upg, T):
    lat, rope = rest[:upg], rest[upg:2 * upg]
    o_ref, m_ref, l_ref, acc_ref = rest[2 * upg:]
    p = pl.program_id(1)
    rows = D_HEADS * T

    @pl.when(p == 0)
    def _():
        m_ref[...] = jnp.full(m_ref.shape, NEG_BIG, F32)
        l_ref[...] = jnp.zeros(l_ref.shape, F32)
        acc_ref[...] = jnp.zeros(acc_ref.shape, F32)

    ql, qr = ql_ref[...].astype(BF16), qr_ref[...]

    c_all = jnp.concatenate([x[...].astype(BF16) for x in lat], axis=0)
    rt_all = jnp.concatenate([x[...].astype(BF16) for x in rope], axis=1)
    s = (lax.dot_general(ql, c_all, _NT, preferred_element_type=F32)
         + jnp.dot(qr, rt_all, preferred_element_type=F32)) * MLA_SCALE
    _online_update(m_ref, l_ref, acc_ref, s, lambda pb: jnp.dot(pb, c_all, preferred_element_type=F32))

    @pl.when(p == pl.num_programs(1) - 1)
    def _():
        cn = cn_ref[...]
        sn = (lax.dot_general(ql, cn, _NT, preferred_element_type=F32)
              + lax.dot_general(qr, rn_ref[...], _NT, preferred_element_type=F32)) * MLA_SCALE
        jn = lax.broadcasted_iota(I32, (T, LANES), 1)
        tn = lax.broadcasted_iota(I32, (T, LANES), 0)
        sn = jnp.where(((jn <= tn) & (jn < T))[None], sn.reshape(D_HEADS, T, LANES), -jnp.inf).reshape(rows, LANES)
        _online_update(m_ref, l_ref, acc_ref, sn, lambda pb: jnp.dot(pb, cn, preferred_element_type=F32))
        o_ref[...] = (acc_ref[...] / l_ref[...]).astype(o_ref.dtype)


def mla_sample_attn(pt, q_lat, q_rope, c_new, r_new, lat_pool, rope_pool_t, *, npg, upg, T):
    nb = q_lat.shape[0]
    rows = D_HEADS * T
    per_seq = lambda shape: pl.BlockSpec((None,) + shape, lambda b, p, pt: (b, 0, 0))
    return pl.pallas_call(
        functools.partial(_mla_sample_kernel, upg=upg, T=T),
        grid_spec=pltpu.PrefetchScalarGridSpec(
            num_scalar_prefetch=1, grid=(nb, npg // upg),
            in_specs=[per_seq((rows, D_KV_LORA)), per_seq((rows, D_ROPE)),
                      per_seq((LANES, D_KV_LORA)), per_seq((LANES, D_ROPE))]
                     + _page_specs((D_KV_LORA,), npg, upg)
                     + [pl.BlockSpec((None, D_ROPE, PAGE), lambda b, p, pt, u=u: (pt[b * npg + p * upg + u], 0, 0))
                        for u in range(upg)],
            out_specs=per_seq((rows, D_KV_LORA)),
            scratch_shapes=[pltpu.VMEM((rows, 1), F32), pltpu.VMEM((rows, 1), F32),
                            pltpu.VMEM((rows, D_KV_LORA), F32)]),
        out_shape=jax.ShapeDtypeStruct((nb, rows, D_KV_LORA), F32),
        compiler_params=_params(("parallel", "arbitrary")),
        name="mla_sample_attn",
    )(pt, q_lat, q_rope, c_new, r_new, *([lat_pool] * upg), *([rope_pool_t] * upg))


def _head_mm_kernel(a_ref, w_ref, o_ref):
    nb, t, ka = a_ref.shape
    out = jnp.dot(a_ref[...].reshape(nb * t, ka).astype(BF16), w_ref[...], preferred_element_type=F32)
    o_ref[...] = out.reshape(nb, t, out.shape[1]).astype(o_ref.dtype)


def head_mm(a, w, T, out_dtype):
    nb, _, ka = a.shape
    nh, _, nbo = w.shape
    return pl.pallas_call(
        _head_mm_kernel,
        grid=(nh,),
        in_specs=[pl.BlockSpec((nb, T, ka), lambda h: (0, h, 0)),
                  pl.BlockSpec((None, ka, nbo), lambda h: (h, 0, 0))],
        out_specs=pl.BlockSpec((nb, T, nbo), lambda h: (0, h, 0)),
        out_shape=jax.ShapeDtypeStruct((nb, nh * T, nbo), out_dtype),
        compiler_params=_params(("parallel",)),
        name="head_mm",
    )(a, w)


def _flags(*runs):
    out = []
    for cnt, flag in runs:
        out += [flag] * cnt
    return jnp.array(out, I32)


def _pad_tokens(a, rows=LANES):
    nb, t, c = a.shape
    return jnp.concatenate([a, jnp.zeros((nb, rows - t, c), a.dtype)], axis=1)


def kernel(x_prompt, x_sample, cache_a_kv, cache_a_idx_k, state_b_kv, state_c_kv_0, state_c_kv_1, state_c_kv_2,
           cache_d_latent, cache_d_k_rope, page_table, norm_mix, norm_ffn, norm_final, a_w_in, a_w_out, b_w_in,
           b_sinks, b_w_out, c_w_in, c_w_out, d_w_in, d_q_norm, d_w_q_up, d_kv_norm, d_w_kv_up, d_w_out,
           ffn_w_in, ffn_w_out):
    B, S, DM = x_prompt.shape
    NB, T, _ = x_sample.shape
    NPG = page_table.shape[1]
    PAST = NPG * PAGE
    MP, MS = B * S, NB * T
    assert T == SUBLANES and S % QBLK == 0 and PAST % 512 == 0
    UPG = _pick(NPG, 16)

    x = jnp.concatenate([x_prompt.reshape(MP, DM), x_sample.reshape(MS, DM)], axis=0)
    pos = jnp.concatenate([jnp.tile(jnp.arange(S), B), jnp.tile(PAST + jnp.arange(T), NB)])
    cos128, sin128 = _rope_tables(pos, 128)
    cos64, sin64 = _rope_tables(pos, 64)
    pt = page_table.reshape(-1).astype(I32)

    def tok_rows(a, heads, dim):
        return a.reshape(NB, T, heads, dim).transpose(0, 2, 1, 3).reshape(NB, heads * T, dim)

    def rows_tok(a, heads, dim):
        return a.reshape(NB, heads, T, dim).transpose(0, 2, 1, 3).reshape(MS, heads * dim)

    ffn_in16, ffn_out16 = ffn_w_in.astype(BF16), ffn_w_out.astype(BF16)

    def layer_tail(att, w_out, layer, xin):
        y = mm_res(att, w_out.astype(BF16), xin)
        return ffn(y, norm_ffn[layer][None], ffn_in16, ffn_out16, layer)

    wq, wkv, wqi, wki, wwi = jnp.split(a_w_in, [2048, 3072, 5120, 5248], axis=1)
    wa = jnp.concatenate([wq, wqi, wkv, wki, wwi, jnp.zeros((DM, A_NP - a_w_in.shape[1]), F32)], axis=1)
    p32, p16 = norm_proj(x, 0, norm_mix[0][None], wa.astype(BF16),
                         _flags((16, 1), (16, 1), (4, 1), (4, 0), (1, 1), (1, 0)), cos128, sin128, 128)
    a_kv_p = p32[:MP, A_K0:A_KI0].reshape(B, S, 2, A_KV_HEADS, A_HEAD_DIM)
    a_kv_s = p32[MP:, A_K0:A_KI0].reshape(NB, T, 2, A_KV_HEADS, A_HEAD_DIM)
    a_ik_p = p32[:MP, A_KI0:A_WI0].reshape(B, S, A_IDX_DIM)
    a_ik_s = p32[MP:, A_KI0:A_WI0].reshape(NB, T, A_IDX_DIM)
    att_p = dsa_prompt(p16, p32, B, S)

    s16, s32 = p16[MP:], p32[MP:]
    qi_s = tok_rows(s16[:, A_QI0:A_K0], A_IDX_HEADS, A_IDX_DIM)
    w_s = s32[:, A_WI0:A_WI0 + A_IDX_HEADS].reshape(NB, T, A_IDX_HEADS).transpose(0, 2, 1)
    w_s = jnp.broadcast_to(w_s.reshape(NB, A_IDX_HEADS * T, 1), (NB, A_IDX_HEADS * T, LANES))
    scores = dsa_sample_scores(pt, qi_s, w_s, cache_a_idx_k, npg=NPG, upg=UPG, T=T)
    ki_new = _pad_tokens(s16[:, A_KI0:A_WI0].reshape(NB, T, A_IDX_DIM))
    thr, sel_new = dsa_sample_threshold(scores, qi_s, w_s, ki_new, T=T)
    kvd = A_KV_HEADS * A_HEAD_DIM
    att_s = dsa_sample_attn(
        pt, tok_rows(s16[:, A_Q0:A_QI0], A_HEADS, A_HEAD_DIM), scores, thr, sel_new,
        _pad_tokens(s16[:, A_K0:A_V0].reshape(NB, T, kvd)), _pad_tokens(s16[:, A_V0:A_KI0].reshape(NB, T, kvd)),
        cache_a_kv.reshape(cache_a_kv.shape[0], PAGE * 2 * A_KV_HEADS, A_HEAD_DIM), npg=NPG, upg=UPG, T=T)
    att = jnp.concatenate([att_p, rows_tok(att_s, A_HEADS, A_HEAD_DIM)], axis=0)
    x = layer_tail(att, a_w_out, 0, x)

    hb, db, kb = B_HEADS, B_HEAD_DIM, B_KV_HEADS
    gb = hb // kb
    p32, p16 = norm_proj(x, 0, norm_mix[1][None], b_w_in.astype(BF16), _flags((16, 1), (2, 1), (2, 0)),
                         cos64, sin64, 64)
    q_w, kv_w = hb * db, 2 * kb * db
    att_p = swa_prompt_attn(p16, b_sinks, q_w, B, S)
    b_kv_p = p32[:MP, q_w:q_w + kv_w].reshape(B, S, 2, kb, db)[:, S - min(B_WINDOW, S):]
    s16, s32 = p16[MP:], p32[MP:]
    kvn = s32[:, q_w:q_w + kv_w].reshape(NB, T, 2, kb, db)
    q_s = s16[:, :q_w].reshape(NB, T, kb, gb, db).transpose(0, 2, 3, 1, 4).reshape(NB, kb, gb * T, db)
    o_s, _, b_kv_s = win_sample_attn(q_s, state_b_kv, kvn, sb=8, window=B_WINDOW, dil=1, past=PAST,
                                     scale=db ** -0.5, sinks=b_sinks)
    o_s = o_s.reshape(NB, kb, gb, T, db).transpose(0, 3, 1, 2, 4).reshape(MS, q_w)
    att = jnp.concatenate([att_p, o_s.astype(BF16)], axis=0)
    x = layer_tail(att, b_w_out, 1, x)

    hc, dc, kc = C_HEADS, C_HEAD_DIM, C_KV_HEADS
    gc = hc // kc
    gw = (hc + 2 * kc) * dc
    p32, p16 = norm_proj(x, 0, norm_mix[2][None], c_w_in.astype(BF16),
                         _flags(*([(hc, 1), (kc, 1), (kc, 0)] * len(C_GROUPS))), cos128, sin128, 128)
    s16, s32 = p16[MP:], p32[MP:]
    ncol = p16.shape[1]
    outs, lses, c_states = [], [], []
    for gi, ((window, dil), buf) in enumerate(zip(C_GROUPS, (state_c_kv_0, state_c_kv_1, state_c_kv_2))):
        q0, k0 = gi * gw, gi * gw + hc * dc
        v0 = k0 + kc * dc
        o_p, lse_p = band_attn(
            p16, p16, p16, nb=B, s=S, d=dil, Hkv=kc, G=gc, Dh=dc, Dv=dc,
            q_cb=lambda r, h, q0=q0: (r * ncol + q0) // (gc * dc) + h,
            k_cb=lambda r, h, k0=k0: (r * ncol + k0) // dc + h,
            v_cb=lambda r, h, v0=v0: (r * ncol + v0) // dc + h,
            heads_in_grid=True, ck=QBLK, window=window // dil, scale=dc ** -0.5, want_lse=True, out_dtype=F32)
        c_p = p32[:MP, k0:k0 + 2 * kc * dc].reshape(B, S, 2, kc, dc)[:, S - min(window, S):]
        kvn = s32[:, k0:k0 + 2 * kc * dc].reshape(NB, T, 2, kc, dc)
        q_s = s16[:, q0:k0].reshape(NB, T, kc, gc, dc).transpose(0, 2, 3, 1, 4).reshape(NB, kc, gc * T, dc)
        o_s, lse_s, c_s = win_sample_attn(q_s, buf, kvn, sb=8 if buf.shape[1] <= 512 else 2, window=window,
                                          dil=dil, past=PAST, scale=dc ** -0.5)
        untok = lambda a: a.reshape(NB, kc, gc, T, dc).transpose(0, 3, 1, 2, 4).reshape(MS, hc * dc)
        outs.append(jnp.concatenate([o_p, untok(o_s)], axis=0))
        lses.append(jnp.concatenate([lse_p, untok(lse_s)], axis=0))
        c_states += [c_p, c_s]
    x = merge_mm_res(outs, lses, c_w_out.astype(BF16), x)
    x = ffn(x, norm_ffn[2][None], ffn_in16, ffn_out16, 2)

    hd = D_HEADS
    dpad = 2 * LANES
    wd = jnp.concatenate([d_w_in, jnp.zeros((DM, LANES - D_ROPE), F32)], axis=1)
    d32, d16 = norm_proj(x, 0, norm_mix[3][None], wd.astype(BF16), _flags((8, 0), (1, 1)), cos64, sin64, 64)
    wqu = d_w_q_up.reshape(D_Q_LORA, hd, D_NOPE + D_ROPE)
    wqu = jnp.concatenate([wqu, jnp.zeros((D_Q_LORA, hd, dpad - D_NOPE - D_ROPE), F32)], axis=2)
    q32, q16 = norm_proj(d32, 0, d_q_norm[None], wqu.reshape(D_Q_LORA, hd * dpad).astype(BF16),
                         _flags(*([(1, 0), (1, 1)] * hd)), cos64, sin64, 64)
    c_kv = rmsnorm(d32, 1, d_kv_norm[None])
    d_lat_p = c_kv[:MP].reshape(B, S, D_KV_LORA)
    d_lat_s = c_kv[MP:].reshape(NB, T, D_KV_LORA)
    kr0 = D_Q_LORA + D_KV_LORA
    d_kr_p = d32[:MP, kr0:kr0 + D_ROPE].reshape(B, S, D_ROPE)
    d_kr_s = d32[MP:, kr0:kr0 + D_ROPE].reshape(NB, T, D_ROPE)
    _, kv16 = norm_proj(d32, 1, d_kv_norm[None], d_w_kv_up.astype(BF16), jnp.zeros((hd * 2,), I32),
                        cos64, sin64, 64, rows=MP)
    att_p = mla_prompt_attn(q16, kv16, d16, kr0 // LANES, B, S)
    qs = q32[MP:].reshape(NB, T, hd, dpad)
    qn_s = qs[..., :D_NOPE].transpose(0, 2, 1, 3).reshape(NB, hd * T, D_NOPE)
    qr_s = qs[..., D_NOPE:D_NOPE + D_ROPE].transpose(0, 2, 1, 3).reshape(NB, hd * T, D_ROPE).astype(BF16)
    w_kv = d_w_kv_up.reshape(D_KV_LORA, hd, D_NOPE + D_V)
    w_uk_t = w_kv[..., :D_NOPE].transpose(1, 2, 0).astype(BF16)
    w_uv = w_kv[..., D_NOPE:].transpose(1, 0, 2).astype(BF16)
    q_lat = head_mm(qn_s, w_uk_t, T, F32)
    o_lat = mla_sample_attn(
        pt, q_lat, qr_s, _pad_tokens(c_kv[MP:].reshape(NB, T, D_KV_LORA)).astype(BF16),
        _pad_tokens(d16[MP:, kr0:kr0 + D_ROPE].reshape(NB, T, D_ROPE)),
        cache_d_latent, cache_d_k_rope.transpose(0, 2, 1), npg=NPG, upg=UPG, T=T)
    o_s = head_mm(o_lat, w_uv, T, F32)
    att = jnp.concatenate([att_p, rows_tok(o_s, hd, D_V).astype(BF16)], axis=0)
    x = layer_tail(att, d_w_out, 3, x)

    y = rmsnorm(x, 0, norm_final[None])
    return (y[:MP].reshape(B, S, DM), y[MP:].reshape(NB, T, DM), a_kv_p, a_kv_s, a_ik_p, a_ik_s, b_kv_p, b_kv_s,
            *c_states, d_lat_p, d_lat_s, d_kr_p, d_kr_s)
```
